```python
import jax, jax.numpy as jnp
from jax import lax
import numpy as np

D_MODEL = 1024
BATCH = 4
SEQ = 8192
DEPTH = 4
DEC_BATCH = 8
DEC_SEQ = 2048
PAST_LEN = 128

N_MIXERS = 2
N_GLA_LAYERS = (DEPTH + 1) // 2
N_SWA_LAYERS = DEPTH // 2
N_NORMS = 6
NORM_EPS = 1e-6

D_FF = 2816
FFN_RES = 0.5

GLA_HEADS = 4
GLA_DK = D_MODEL // 2 // GLA_HEADS
GLA_DV = D_MODEL // GLA_HEADS
GLA_QK = GLA_HEADS * GLA_DK
GLA_V = GLA_HEADS * GLA_DV
GLA_GATE_RANK = 16
GLA_TAU = 16.0
GLA_CHUNK = 64
GLA_IN = 2 * GLA_QK + 2 * GLA_V + 2 * GLA_GATE_RANK

SWA_Q_HEADS = 16
SWA_KV_HEADS = 4
SWA_GROUP = SWA_Q_HEADS // SWA_KV_HEADS
SWA_HD = 64
SWA_WINDOW = 128
SWA_BLOCK = 128
SWA_IN = (SWA_Q_HEADS + 2 * SWA_KV_HEADS) * SWA_HD
ROPE_THETA = 500000.0
ROPE_DIM = SWA_HD // 4
NEG_BIG = -1e30

kernel_name = "hybrid_gla_swa_macaron_encoder"


def rmsnorm(x, g):
    xf = x.astype(jnp.float32)
    xf = xf * lax.rsqrt(jnp.mean(xf * xf, axis=-1, keepdims=True) + NORM_EPS)
    return (xf * g.astype(jnp.float32)).astype(x.dtype)


def swiglu(x, w1, w2):
    gate, up = jnp.split(x @ w1, 2, axis=-1)
    return (jax.nn.silu(gate) * up) @ w2


def gla_chunked_causal(q, k, v, g):
    Bn, L, H, DK = q.shape
    DV = v.shape[-1]
    C = GLA_CHUNK
    N = L // C
    q = q.reshape(Bn, N, C, H, DK)
    k = k.reshape(Bn, N, C, H, DK)
    v = v.reshape(Bn, N, C, H, DV)
    b = jnp.cumsum(g.reshape(Bn, N, C, H, DK), axis=2)
    b_last = b[:, :, -1]
    q_dec = q * jnp.exp(b)
    k_inv = k * jnp.exp(-b)
    k_end = k * jnp.exp(b_last[:, :, None] - b)
    mask = jnp.tril(jnp.ones((C, C), dtype=bool))
    a = jnp.einsum('bnchk,bnshk->bnhcs', q_dec, k_inv)
    a = jnp.where(mask, a, 0.0)
    o_intra = jnp.einsum('bnhcs,bnshv->bnchv', a, v)

    def step(S, inp):
        q_c, k_c, v_c, dec = inp
        o = jnp.einsum('bchk,bhkv->bchv', q_c, S)
        S = dec[..., None] * S + jnp.einsum('bchk,bchv->bhkv', k_c, v_c)
        return S, o

    S0 = jnp.zeros((Bn, H, DK, DV), jnp.float32)
    xs = (jnp.moveaxis(q_dec, 1, 0), jnp.moveaxis(k_end, 1, 0),
          jnp.moveaxis(v, 1, 0), jnp.moveaxis(jnp.exp(b_last), 1, 0))
    _, o_inter = lax.scan(step, S0, xs)
    o = o_intra + jnp.moveaxis(o_inter, 0, 1)
    return o.reshape(Bn, L, H, DV)


def gla_mixer(x, w_in, w_gate_f, b_gate_f, w_gate_b, b_gate_b, g_onorm, w_out):
    Bn, L, _ = x.shape
    h = x @ w_in
    splits = [GLA_QK, 2 * GLA_QK, 2 * GLA_QK + GLA_V, 2 * GLA_QK + 2 * GLA_V,
              2 * GLA_QK + 2 * GLA_V + GLA_GATE_RANK]
    q, k, v, r, gd_f, gd_b = jnp.split(h, splits, axis=-1)
    q = q.reshape(Bn, L, GLA_HEADS, GLA_DK).astype(jnp.float32) * (GLA_DK ** -0.5)
    k = k.reshape(Bn, L, GLA_HEADS, GLA_DK).astype(jnp.float32)
    v = v.reshape(Bn, L, GLA_HEADS, GLA_DV).astype(jnp.float32)
    g_f = jax.nn.log_sigmoid((gd_f @ w_gate_f + b_gate_f).astype(jnp.float32)) / GLA_TAU
    g_b = jax.nn.log_sigmoid((gd_b @ w_gate_b + b_gate_b).astype(jnp.float32)) / GLA_TAU
    g_f = g_f.reshape(Bn, L, GLA_HEADS, GLA_DK)
    g_b = g_b.reshape(Bn, L, GLA_HEADS, GLA_DK)
    o_f = gla_chunked_causal(q, k, v, g_f)
    o_b = jnp.flip(gla_chunked_causal(jnp.flip(q, 1), jnp.flip(k, 1), jnp.flip(v, 1),
                                      jnp.flip(g_b, 1)), axis=1)
    o = o_f + o_b
    o = o * lax.rsqrt(jnp.mean(o * o, axis=-1, keepdims=True) + NORM_EPS)
    o = o.reshape(Bn, L, GLA_V) * g_onorm.astype(jnp.float32)
    o = o.astype(x.dtype) * jax.nn.silu(r)
    return o @ w_out


def rope_partial(x, pos):
    half = ROPE_DIM // 2
    inv_freq = ROPE_THETA ** (-(jnp.arange(half, dtype=jnp.float32) * 2.0 / ROPE_DIM))
    ang = pos.astype(jnp.float32)[:, None] * inv_freq[None, :]
    cos = jnp.cos(ang)[None, :, None, :]
    sin = jnp.sin(ang)[None, :, None, :]
    xr = x[..., :ROPE_DIM].astype(jnp.float32)
    x1, x2 = xr[..., :half], xr[..., half:]
    rot = jnp.concatenate([x1 * cos - x2 * sin, x2 * cos + x1 * sin], axis=-1)
    return jnp.concatenate([rot.astype(x.dtype), x[..., ROPE_DIM:]], axis=-1)


def swa_mixer(x, w_in, sinks, w_out):
    Bn, L, _ = x.shape
    h = x @ w_in
    q, k, v = jnp.split(h, [SWA_Q_HEADS * SWA_HD, (SWA_Q_HEADS + SWA_KV_HEADS) * SWA_HD], axis=-1)
    q = q.reshape(Bn, L, SWA_Q_HEADS, SWA_HD)
    k = k.reshape(Bn, L, SWA_KV_HEADS, SWA_HD)
    v = v.reshape(Bn, L, SWA_KV_HEADS, SWA_HD)
    pos = jnp.arange(L)
    q = rope_partial(q, pos) * (SWA_HD ** -0.5)
    k = rope_partial(k, pos)
    N = L // SWA_BLOCK
    q_blk = q.reshape(Bn, N, SWA_BLOCK, SWA_KV_HEADS, SWA_GROUP, SWA_HD)
    pad = ((0, 0), (SWA_BLOCK, SWA_BLOCK), (0, 0), (0, 0))
    k_pad = jnp.pad(k, pad).reshape(Bn, N + 2, SWA_BLOCK, SWA_KV_HEADS, SWA_HD)
    v_pad = jnp.pad(v, pad).reshape(Bn, N + 2, SWA_BLOCK, SWA_KV_HEADS, SWA_HD)
    k_band = jnp.concatenate([k_pad[:, 0:N], k_pad[:, 1:N + 1], k_pad[:, 2:N + 2]], axis=2)
    v_band = jnp.concatenate([v_pad[:, 0:N], v_pad[:, 1:N + 1], v_pad[:, 2:N + 2]], axis=2)
    s = jnp.einsum('bnqhgd,bnkhd->bnhgqk', q_blk, k_band).astype(jnp.float32)
    blk = jnp.arange(N)[:, None] * SWA_BLOCK
    qpos = blk + jnp.arange(SWA_BLOCK)[None, :]
    kpos = blk - SWA_BLOCK + jnp.arange(3 * SWA_BLOCK)[None, :]
    valid = (jnp.abs(qpos[:, :, None] - kpos[:, None, :]) <= SWA_WINDOW) \
        & ((kpos >= 0) & (kpos < L))[:, None, :]
    s = jnp.where(valid[None, :, None, None, :, :], s, NEG_BIG)
    sink = jnp.broadcast_to(
        sinks.astype(jnp.float32).reshape(1, 1, SWA_KV_HEADS, SWA_GROUP, 1, 1), s.shape[:-1] + (1,))
    p = jax.nn.softmax(jnp.concatenate([s, sink], axis=-1), axis=-1)[..., :-1]
    o = jnp.einsum('bnhgqk,bnkhd->bnqhgd', p.astype(v.dtype), v_band)
    o = o.reshape(Bn, L, SWA_Q_HEADS * SWA_HD)
    return o @ w_out


def trunk(x, norm_g, ffn_w1, ffn_w2, gla_w_in, gla_w_gate_f, gla_b_gate_f, gla_w_gate_b,
          gla_b_gate_b, gla_onorm, gla_w_out, swa_w_in, swa_sinks, swa_w_out):
    for i in range(DEPTH):
        g = norm_g[i]
        h = swiglu(rmsnorm(x, g[0]), ffn_w1[i, 0], ffn_w2[i, 0])
        x = x + FFN_RES * rmsnorm(h, g[1])
        h = rmsnorm(x, g[2])
        j = i // N_MIXERS
        if i % N_MIXERS == 0:
            h = gla_mixer(h, gla_w_in[j], gla_w_gate_f[j], gla_b_gate_f[j], gla_w_gate_b[j],
                          gla_b_gate_b[j], gla_onorm[j], gla_w_out[j])
        else:
            h = swa_mixer(h, swa_w_in[j], swa_sinks[j], swa_w_out[j])
        x = x + rmsnorm(h, g[3])
        h = swiglu(rmsnorm(x, g[4]), ffn_w1[i, 1], ffn_w2[i, 1])
        x = x + FFN_RES * rmsnorm(h, g[5])
    return x


def setup_inputs(seed: int = 0) -> dict:
    key = jax.random.key(seed)
    ks = jax.random.split(key, 16)
    f32 = jnp.float32
    nrm = lambda k, shape, scale: jax.random.normal(k, shape, f32) * scale
    return {
        "x_prompt": nrm(ks[0], (BATCH, SEQ, D_MODEL), 1.0),
        "x_sample": nrm(ks[1], (DEC_BATCH, DEC_SEQ, D_MODEL), 1.0),
        "norm_g": 1.0 + nrm(ks[2], (DEPTH, N_NORMS, D_MODEL), 0.05),
        "ffn_w1": nrm(ks[3], (DEPTH, 2, D_MODEL, 2 * D_FF), D_MODEL ** -0.5),
        "ffn_w2": nrm(ks[4], (DEPTH, 2, D_FF, D_MODEL), D_FF ** -0.5),
        "gla_w_in": nrm(ks[5], (N_GLA_LAYERS, D_MODEL, GLA_IN), D_MODEL ** -0.5),
        "gla_w_gate_f": nrm(ks[6], (N_GLA_LAYERS, GLA_GATE_RANK, GLA_QK), GLA_GATE_RANK ** -0.5),
        "gla_b_gate_f": nrm(ks[7], (N_GLA_LAYERS, GLA_QK), 0.1),
        "gla_w_gate_b": nrm(ks[8], (N_GLA_LAYERS, GLA_GATE_RANK, GLA_QK), GLA_GATE_RANK ** -0.5),
        "gla_b_gate_b": nrm(ks[9], (N_GLA_LAYERS, GLA_QK), 0.1),
        "gla_onorm": 1.0 + nrm(ks[10], (N_GLA_LAYERS, GLA_V), 0.05),
        "gla_w_out": nrm(ks[11], (N_GLA_LAYERS, GLA_V, D_MODEL), GLA_V ** -0.5),
        "swa_w_in": nrm(ks[12], (N_SWA_LAYERS, D_MODEL, SWA_IN), D_MODEL ** -0.5),
        "swa_sinks": nrm(ks[13], (N_SWA_LAYERS, SWA_Q_HEADS), 1.0),
        "swa_w_out": nrm(ks[14], (N_SWA_LAYERS, SWA_Q_HEADS * SWA_HD, D_MODEL), (SWA_Q_HEADS * SWA_HD) ** -0.5),
    }


def reference(x_prompt, x_sample, norm_g, ffn_w1, ffn_w2, gla_w_in, gla_w_gate_f, gla_b_gate_f,
              gla_w_gate_b, gla_b_gate_b, gla_onorm, gla_w_out, swa_w_in, swa_sinks, swa_w_out):
    y_prompt = trunk(x_prompt, norm_g, ffn_w1, ffn_w2, gla_w_in, gla_w_gate_f, gla_b_gate_f,
                     gla_w_gate_b, gla_b_gate_b, gla_onorm, gla_w_out, swa_w_in, swa_sinks, swa_w_out)
    y_sample = trunk(x_sample, norm_g, ffn_w1, ffn_w2, gla_w_in, gla_w_gate_f, gla_b_gate_f,
                     gla_w_gate_b, gla_b_gate_b, gla_onorm, gla_w_out, swa_w_in, swa_sinks, swa_w_out)
    return (y_prompt, y_sample)
```

```python
import functools

import jax
import jax.numpy as jnp
from jax import lax
from jax.experimental import pallas as pl
from jax.experimental.pallas import tpu as pltpu

F32 = jnp.float32
BF16 = jnp.bfloat16

D_MODEL = 1024
DEPTH = 4
NORM_EPS = 1e-6

D_FF = 2816
FFN_RES = 0.5
FFN_CHUNK = 256
FFN_NCHUNK = D_FF // FFN_CHUNK

GLA_HEADS = 4
GLA_DK = 128
GLA_DV = 256
GLA_QK = GLA_HEADS * GLA_DK
GLA_V = GLA_HEADS * GLA_DV
GLA_GATE_RANK = 16
GLA_TAU = 16.0
GLA_C = 128
GLA_MID = GLA_C // 2 - 1

SWA_Q_HEADS = 16
SWA_KV_HEADS = 4
SWA_GROUP = SWA_Q_HEADS // SWA_KV_HEADS
SWA_HD = 64
SWA_WINDOW = 128
SWA_BLOCK = 128
SWA_Q = SWA_Q_HEADS * SWA_HD
SWA_KV = SWA_KV_HEADS * SWA_HD
ROPE_THETA = 500000.0
ROPE_DIM = SWA_HD // 4
ROPE_HALF = ROPE_DIM // 2
NEG_BIG = -1e30

LANES = 128
TOKEN_TILE = 512
GLA_REC_TILE = 256
VMEM_LIMIT = 56 * 1024 * 1024


def _rms(x, g):
    ms = jnp.mean(x * x, axis=-1, keepdims=True)
    return x * lax.rsqrt(ms + NORM_EPS) * g


def _silu(x):
    return x * (1.0 / (1.0 + jnp.exp(-x)))


def _const_spec(shape):
    nd = len(shape)
    return pl.BlockSpec(shape, lambda *_: (0,) * nd, pipeline_mode=pl.Buffered(1))


def _params(sem):
    return pltpu.CompilerParams(dimension_semantics=sem, vmem_limit_bytes=VMEM_LIMIT)


def _ffn_body(x_ref, g_ref, w1_ref, w2_ref, o_ref, h_ref):
    x = x_ref[...]
    xn = _rms(x, g_ref[0:1, :]).astype(BF16)
    for c in range(FFN_NCHUNK):
        gu = jnp.dot(xn, w1_ref[:, c * 2 * FFN_CHUNK:(c + 1) * 2 * FFN_CHUNK],
                     preferred_element_type=F32)
        gate = gu[:, :FFN_CHUNK]
        up = gu[:, FFN_CHUNK:]
        h_ref[:, c * FFN_CHUNK:(c + 1) * FFN_CHUNK] = (_silu(gate) * up).astype(BF16)
    y = jnp.dot(h_ref[...], w2_ref[...], preferred_element_type=F32)
    o_ref[...] = x + FFN_RES * _rms(y, g_ref[1:2, :])


def _ffn(x2, g2, w1r, w2):
    t = x2.shape[0]
    tm = TOKEN_TILE
    return pl.pallas_call(
        _ffn_body,
        grid=(t // tm,),
        in_specs=[
            pl.BlockSpec((tm, D_MODEL), lambda i: (i, 0)),
            _const_spec((2, D_MODEL)),
            _const_spec((D_MODEL, 2 * D_FF)),
            _const_spec((D_FF, D_MODEL)),
        ],
        out_specs=pl.BlockSpec((tm, D_MODEL), lambda i: (i, 0)),
        out_shape=jax.ShapeDtypeStruct((t, D_MODEL), F32),
        scratch_shapes=[pltpu.VMEM((tm, D_FF), BF16)],
        compiler_params=_params(("parallel",)),
        name="ffn",
    )(x2, g2, w1r, w2)


def _gla_proj_body(x_ref, g_ref, wm_ref, wgd_ref, wgate_ref, bgate_ref,
                   qkf_ref, qkb_ref, v_ref, r_ref, dl_ref):
    tm = x_ref.shape[0]
    xn = _rms(x_ref[...], g_ref[...]).astype(BF16)
    h = jnp.dot(xn, wm_ref[...], preferred_element_type=F32)
    v_ref[...] = h[:, 2 * GLA_QK:2 * GLA_QK + GLA_V].astype(BF16)
    r_ref[...] = h[:, 2 * GLA_QK + GLA_V:].astype(BF16)
    gd = jnp.dot(xn, wgd_ref[...], preferred_element_type=F32).astype(BF16)
    z = jnp.dot(gd, wgate_ref[...], preferred_element_type=F32) + bgate_ref[...]
    logdecay = (jnp.minimum(z, 0.0) - jnp.log(1.0 + jnp.exp(-jnp.abs(z)))) * (1.0 / GLA_TAU)

    row = lax.broadcasted_iota(jnp.int32, (GLA_C, GLA_C), 0)
    col = lax.broadcasted_iota(jnp.int32, (GLA_C, GLA_C), 1)
    tri_f = (col <= row).astype(BF16)
    tri_b = (col >= row).astype(BF16)

    for c in range(tm // GLA_C):
        rows = slice(c * GLA_C, (c + 1) * GLA_C)
        q = h[rows, :GLA_QK] * (GLA_DK ** -0.5)
        k = h[rows, GLA_QK:2 * GLA_QK]
        for d, (tri, out_ref, last, mid) in enumerate((
                (tri_f, qkf_ref, GLA_C - 1, GLA_MID),
                (tri_b, qkb_ref, 0, GLA_MID + 1))):
            g = logdecay[rows, d * GLA_QK:(d + 1) * GLA_QK]
            g_hi = g.astype(BF16)
            g_lo = (g - g_hi.astype(F32)).astype(BF16)
            b = (jnp.dot(tri, g_hi, preferred_element_type=F32)
                 + jnp.dot(tri, g_lo, preferred_element_type=F32))
            b_mid = b[mid:mid + 1, :]
            b_last = b[last:last + 1, :]
            out_ref[rows, 0 * GLA_QK:1 * GLA_QK] = (q * jnp.exp(b - b_mid)).astype(BF16)
            out_ref[rows, 1 * GLA_QK:2 * GLA_QK] = (q * jnp.exp(b)).astype(BF16)
            out_ref[rows, 2 * GLA_QK:3 * GLA_QK] = (k * jnp.exp(b_mid - b)).astype(BF16)
            out_ref[rows, 3 * GLA_QK:4 * GLA_QK] = (k * jnp.exp(b_last - b)).astype(BF16)
            dl_ref[c, :, d * GLA_QK:(d + 1) * GLA_QK] = jnp.exp(b_last)


def _gla_proj(x2, g_pre, wm, wgd, wgate, bgate):
    t = x2.shape[0]
    tm = TOKEN_TILE
    nc = tm // GLA_C
    tok = lambda w: pl.BlockSpec((tm, w), lambda i: (i, 0))
    return pl.pallas_call(
        _gla_proj_body,
        grid=(t // tm,),
        in_specs=[
            tok(D_MODEL),
            _const_spec((1, D_MODEL)),
            _const_spec((D_MODEL, 2 * GLA_QK + 2 * GLA_V)),
            _const_spec((D_MODEL, LANES)),
            _const_spec((LANES, 2 * GLA_QK)),
            _const_spec((1, 2 * GLA_QK)),
        ],
        out_specs=[
            tok(4 * GLA_QK), tok(4 * GLA_QK), tok(GLA_V), tok(GLA_V),
            pl.BlockSpec((nc, 1, 2 * GLA_QK), lambda i: (i, 0, 0)),
        ],
        out_shape=[
            jax.ShapeDtypeStruct((t, 4 * GLA_QK), BF16),
            jax.ShapeDtypeStruct((t, 4 * GLA_QK), BF16),
            jax.ShapeDtypeStruct((t, GLA_V), BF16),
            jax.ShapeDtypeStruct((t, GLA_V), BF16),
            jax.ShapeDtypeStruct((t // GLA_C, 1, 2 * GLA_QK), F32),
        ],
        compiler_params=_params(("parallel",)),
        name="gla_proj",
    )(x2, g_pre, wm, wgd, wgate, bgate)


def _gla_rec_body(qkf_ref, qkb_ref, vf_ref, vb_ref, dlf_ref, dlb_ref, of_ref, ob_ref, s_ref):
    tb = qkf_ref.shape[1]
    nck = tb // GLA_C

    @pl.when(pl.program_id(1) == 0)
    def _():
        s_ref[...] = jnp.zeros_like(s_ref)

    row = lax.broadcasted_iota(jnp.int32, (GLA_C, GLA_C), 0)
    col = lax.broadcasted_iota(jnp.int32, (GLA_C, GLA_C), 1)

    def unit(qk_ref, v_ref, dl_ref, o_ref, mask, c, head, d):
        rows = slice(c * GLA_C, (c + 1) * GLA_C)
        lanes = lambda part: slice(part * GLA_QK + head * GLA_DK, part * GLA_QK + (head + 1) * GLA_DK)
        q_mid = qk_ref[0, rows, lanes(0)]
        q_dec = qk_ref[0, rows, lanes(1)]
        k_mid = qk_ref[0, rows, lanes(2)]
        k_end = qk_ref[0, rows, lanes(3)]
        vv = v_ref[0, rows, head * GLA_DV:(head + 1) * GLA_DV]
        state = s_ref[d * GLA_HEADS + head]
        scores = lax.dot_general(q_mid, k_mid, (((1,), (1,)), ((), ())), preferred_element_type=F32)
        a = jnp.where(mask, scores, 0.0).astype(BF16)
        lhs = jnp.concatenate([a, q_dec], axis=1)
        rhs = jnp.concatenate([vv, state.astype(BF16)], axis=0)
        o_ref[0, rows, head * GLA_DV:(head + 1) * GLA_DV] = jnp.dot(lhs, rhs, preferred_element_type=F32)
        kv = lax.dot_general(k_end, vv, (((0,), (0,)), ((), ())), preferred_element_type=F32)
        dl_row = dl_ref[0, c, :, d * GLA_QK + head * GLA_DK:d * GLA_QK + (head + 1) * GLA_DK]
        dl_col = jnp.transpose(jnp.broadcast_to(dl_row, (GLA_DK, GLA_DK)))
        s_ref[d * GLA_HEADS + head] = jnp.concatenate([dl_col, dl_col], axis=1) * state + kv

    for c in range(nck):
        for head in range(GLA_HEADS):
            unit(qkf_ref, vf_ref, dlf_ref, of_ref, col <= row, c, head, 0)
    for c in reversed(range(nck)):
        for head in range(GLA_HEADS):
            unit(qkb_ref, vb_ref, dlb_ref, ob_ref, col >= row, c, head, 1)


def _gla_rec(qkf, qkb, v, dl, bsz, seq):
    tb = GLA_REC_TILE
    nb = seq // tb
    nck = tb // GLA_C
    fwd = lambda b, i: (b, i, 0)
    bwd = lambda b, i: (b, nb - 1 - i, 0)
    return pl.pallas_call(
        _gla_rec_body,
        grid=(bsz, nb),
        in_specs=[
            pl.BlockSpec((1, tb, 4 * GLA_QK), fwd),
            pl.BlockSpec((1, tb, 4 * GLA_QK), bwd),
            pl.BlockSpec((1, tb, GLA_V), fwd),
            pl.BlockSpec((1, tb, GLA_V), bwd),
            pl.BlockSpec((1, nck, 1, 2 * GLA_QK), lambda b, i: (b, i, 0, 0)),
            pl.BlockSpec((1, nck, 1, 2 * GLA_QK), lambda b, i: (b, nb - 1 - i, 0, 0)),
        ],
        out_specs=[
            pl.BlockSpec((1, tb, GLA_V), fwd),
            pl.BlockSpec((1, tb, GLA_V), bwd),
        ],
        out_shape=[
            jax.ShapeDtypeStruct((bsz, seq, GLA_V), F32),
            jax.ShapeDtypeStruct((bsz, seq, GLA_V), F32),
        ],
        scratch_shapes=[pltpu.VMEM((2 * GLA_HEADS, GLA_DK, GLA_DV), F32)],
        compiler_params=_params(("parallel", "arbitrary")),
        name="gla_rec",
    )(qkf, qkb, v, v, dl, dl)


def _gla_out_body(of_ref, ob_ref, r_ref, x_ref, gon_ref, gpost_ref, wout_ref, y_ref):
    o = of_ref[...] + ob_ref[...]
    parts = []
    for head in range(GLA_HEADS):
        oh = o[:, head * GLA_DV:(head + 1) * GLA_DV]
        ms = jnp.mean(oh * oh, axis=-1, keepdims=True)
        parts.append(oh * lax.rsqrt(ms + NORM_EPS))
    on = jnp.concatenate(parts, axis=1) * gon_ref[...]
    gated = (on * _silu(r_ref[...].astype(F32))).astype(BF16)
    y = jnp.dot(gated, wout_ref[...], preferred_element_type=F32)
    y_ref[...] = x_ref[...] + _rms(y, gpost_ref[...])


def _gla_out(o_f, o_b, r, x2, g_onorm, g_post, w_out):
    t = x2.shape[0]
    tm = TOKEN_TILE
    tok = lambda: pl.BlockSpec((tm, D_MODEL), lambda i: (i, 0))
    return pl.pallas_call(
        _gla_out_body,
        grid=(t // tm,),
        in_specs=[tok(), tok(), tok(), tok(),
                  _const_spec((1, GLA_V)), _const_spec((1, D_MODEL)), _const_spec((GLA_V, D_MODEL))],
        out_specs=tok(),
        out_shape=jax.ShapeDtypeStruct((t, D_MODEL), F32),
        compiler_params=_params(("parallel",)),
        name="gla_out",
    )(o_f, o_b, r, x2, g_onorm, g_post, w_out)


def _gla_layer(x, g_pre, g_post, w):
    bsz, seq, _ = x.shape
    x2 = x.reshape(bsz * seq, D_MODEL)
    qkf, qkb, v, r, dl = _gla_proj(x2, g_pre, w["wm"], w["wgd"], w["wgate"], w["bgate"])
    shp = lambda a: a.reshape(bsz, seq, a.shape[-1])
    o_f, o_b = _gla_rec(shp(qkf), shp(qkb), shp(v), dl.reshape(bsz, seq // GLA_C, 1, 2 * GLA_QK), bsz, seq)
    y = _gla_out(o_f.reshape(bsz * seq, GLA_V), o_b.reshape(bsz * seq, GLA_V), r, x2,
                 w["onorm"], g_post, w["wout"])
    return y.reshape(bsz, seq, D_MODEL)


def _rope(z, cos, sin_a, sin_b):
    return (z * cos + pltpu.roll(z, LANES - ROPE_HALF, axis=1) * sin_a
            + pltpu.roll(z, ROPE_HALF, axis=1) * sin_b)


def _swa_proj_body(x_ref, g_ref, w_ref, cos_ref, sa_ref, sb_ref, q_ref, k_ref, v_ref):
    xn = _rms(x_ref[0], g_ref[...]).astype(BF16)
    h = jnp.dot(xn, w_ref[...], preferred_element_type=F32)
    cos, sin_a, sin_b = cos_ref[...], sa_ref[...], sb_ref[...]
    for j in range(SWA_Q // LANES):
        blk = slice(j * LANES, (j + 1) * LANES)
        q_ref[0, :, blk] = (_rope(h[:, blk], cos, sin_a, sin_b) * (SWA_HD ** -0.5)).astype(BF16)
    for j in range(SWA_KV // LANES):
        blk = slice(j * LANES, (j + 1) * LANES)
        k_ref[0, :, blk] = _rope(h[:, SWA_Q + j * LANES:SWA_Q + (j + 1) * LANES], cos, sin_a, sin_b).astype(BF16)
    v_ref[0] = h[:, SWA_Q + SWA_KV:].astype(BF16)


def _swa_proj(x, g_pre, w_in, cos, sin_a, sin_b):
    bsz, seq, _ = x.shape
    tm = TOKEN_TILE
    tok = lambda w: pl.BlockSpec((1, tm, w), lambda b, i: (b, i, 0))
    tab = lambda: pl.BlockSpec((tm, LANES), lambda b, i: (i, 0))
    return pl.pallas_call(
        _swa_proj_body,
        grid=(bsz, seq // tm),
        in_specs=[tok(D_MODEL), _const_spec((1, D_MODEL)), _const_spec((D_MODEL, SWA_Q + 2 * SWA_KV)),
                  tab(), tab(), tab()],
        out_specs=[tok(SWA_Q), tok(SWA_KV), tok(SWA_KV)],
        out_shape=[
            jax.ShapeDtypeStruct((bsz, seq, SWA_Q), BF16),
            jax.ShapeDtypeStruct((bsz, seq, SWA_KV), BF16),
            jax.ShapeDtypeStruct((bsz, seq, SWA_KV), BF16),
        ],
        compiler_params=_params(("parallel", "parallel")),
        name="swa_proj",
    )(x, g_pre, w_in, cos, sin_a, sin_b)


def _swa_attn_body(seq, sink_ref, q_ref, kp_ref, kc_ref, kn_ref, vp_ref, vc_ref, vn_ref,
                   x_ref, gpost_ref, wout_ref, y_ref):
    n = pl.program_id(1)
    kb = jnp.concatenate([kp_ref[0], kc_ref[0], kn_ref[0]], axis=0)
    vb = jnp.concatenate([vp_ref[0], vc_ref[0], vn_ref[0]], axis=0)
    qpos = n * SWA_BLOCK + lax.broadcasted_iota(jnp.int32, (SWA_BLOCK, 3 * SWA_BLOCK), 0)
    kpos = (n - 1) * SWA_BLOCK + lax.broadcasted_iota(jnp.int32, (SWA_BLOCK, 3 * SWA_BLOCK), 1)
    valid = (jnp.abs(qpos - kpos) <= SWA_WINDOW) & (kpos >= 0) & (kpos < seq)
    outs = []
    for hq in range(SWA_Q_HEADS):
        hk = hq // SWA_GROUP
        qh = q_ref[0, :, hq * SWA_HD:(hq + 1) * SWA_HD]
        kh = kb[:, hk * SWA_HD:(hk + 1) * SWA_HD]
        vh = vb[:, hk * SWA_HD:(hk + 1) * SWA_HD]
        s = lax.dot_general(qh, kh, (((1,), (1,)), ((), ())), preferred_element_type=F32)
        s = jnp.where(valid, s, NEG_BIG)
        sink = sink_ref[hq]
        m = jnp.maximum(jnp.max(s, axis=-1, keepdims=True), sink)
        p = jnp.exp(s - m)
        den = jnp.sum(p, axis=-1, keepdims=True) + jnp.exp(sink - m)
        o = jnp.dot(p.astype(BF16), vh, preferred_element_type=F32)
        outs.append(o * (1.0 / den))
    o_all = jnp.concatenate(outs, axis=1).astype(BF16)
    y = jnp.dot(o_all, wout_ref[...], preferred_element_type=F32)
    y_ref[0] = x_ref[0] + _rms(y, gpost_ref[...])


def _swa_attn(q, k, v, x, sinks, g_post, w_out):
    bsz, seq, _ = x.shape
    nq = seq // SWA_BLOCK
    cur = lambda b, n: (b, n, 0)
    prev = lambda b, n: (b, jnp.maximum(n - 1, 0), 0)
    nxt = lambda b, n: (b, jnp.minimum(n + 1, nq - 1), 0)
    kv = lambda im: pl.BlockSpec((1, SWA_BLOCK, SWA_KV), im)
    return pl.pallas_call(
        functools.partial(_swa_attn_body, seq),
        grid=(bsz, nq),
        in_specs=[
            pl.BlockSpec(memory_space=pltpu.SMEM),
            pl.BlockSpec((1, SWA_BLOCK, SWA_Q), cur),
            kv(prev), kv(cur), kv(nxt), kv(prev), kv(cur), kv(nxt),
            pl.BlockSpec((1, SWA_BLOCK, D_MODEL), cur),
            _const_spec((1, D_MODEL)),
            _const_spec((SWA_Q, D_MODEL)),
        ],
        out_specs=pl.BlockSpec((1, SWA_BLOCK, D_MODEL), cur),
        out_shape=jax.ShapeDtypeStruct((bsz, seq, D_MODEL), F32),
        compiler_params=_params(("parallel", "parallel")),
        name="swa_attn",
    )(sinks, q, k, k, k, v, v, v, x, g_post, w_out)


def _rope_tables(seq):
    inv_freq = ROPE_THETA ** (-(jnp.arange(ROPE_HALF, dtype=F32) * 2.0 / ROPE_DIM))
    ang = jnp.arange(seq, dtype=F32)[:, None] * inv_freq[None, :]
    cos8, sin8 = jnp.cos(ang), jnp.sin(ang)
    ones = jnp.ones((seq, SWA_HD - ROPE_DIM), F32)
    zeros8 = jnp.zeros((seq, ROPE_HALF), F32)
    zeros = jnp.zeros((seq, SWA_HD - ROPE_DIM), F32)
    cos = jnp.concatenate([cos8, cos8, ones], axis=1)
    sin_a = jnp.concatenate([-sin8, zeros8, zeros], axis=1)
    sin_b = jnp.concatenate([zeros8, sin8, zeros], axis=1)
    rep = LANES // SWA_HD
    return tuple(jnp.tile(t, (1, rep)) for t in (cos, sin_a, sin_b))


def _swa_layer(x, g_pre, g_post, w):
    seq = x.shape[1]
    cos, sin_a, sin_b = _rope_tables(seq)
    q, k, v = _swa_proj(x, g_pre, w["win"], cos, sin_a, sin_b)
    return _swa_attn(q, k, v, x, w["sinks"], g_post, w["wout"])


def _ffn_layer(x, g2, w1r, w2):
    bsz, seq, _ = x.shape
    return _ffn(x.reshape(bsz * seq, D_MODEL), g2, w1r, w2).reshape(bsz, seq, D_MODEL)


def _prep_weights(ffn_w1, ffn_w2, gla_w_in, gla_w_gate_f, gla_b_gate_f, gla_w_gate_b, gla_b_gate_b,
                  gla_onorm, gla_w_out, swa_w_in, swa_sinks, swa_w_out):
    w1r = ffn_w1.reshape(DEPTH, 2, D_MODEL, 2, FFN_NCHUNK, FFN_CHUNK)
    w1r = jnp.transpose(w1r, (0, 1, 2, 4, 3, 5)).reshape(DEPTH, 2, D_MODEL, 2 * D_FF).astype(BF16)
    w2 = ffn_w2.astype(BF16)
    gla = []
    for j in range(gla_w_in.shape[0]):
        n_main = 2 * GLA_QK + 2 * GLA_V
        wgd = jnp.zeros((D_MODEL, LANES), F32).at[:, :2 * GLA_GATE_RANK].set(gla_w_in[j][:, n_main:])
        wgate = jnp.zeros((LANES, 2 * GLA_QK), F32)
        wgate = wgate.at[:GLA_GATE_RANK, :GLA_QK].set(gla_w_gate_f[j])
        wgate = wgate.at[GLA_GATE_RANK:2 * GLA_GATE_RANK, GLA_QK:].set(gla_w_gate_b[j])
        gla.append(dict(
            wm=gla_w_in[j][:, :n_main].astype(BF16),
            wgd=wgd.astype(BF16),
            wgate=wgate.astype(BF16),
            bgate=jnp.concatenate([gla_b_gate_f[j], gla_b_gate_b[j]])[None, :],
            onorm=gla_onorm[j][None, :],
            wout=gla_w_out[j].astype(BF16),
        ))
    swa = []
    for j in range(swa_w_in.shape[0]):
        swa.append(dict(win=swa_w_in[j].astype(BF16), sinks=swa_sinks[j], wout=swa_w_out[j].astype(BF16)))
    return w1r, w2, gla, swa


def _trunk(x, norm_g, w1r, w2, gla, swa):
    for i in range(DEPTH):
        g = norm_g[i]
        x = _ffn_layer(x, g[0:2], w1r[i, 0], w2[i, 0])
        if i % 2 == 0:
            x = _gla_layer(x, g[2:3], g[3:4], gla[i // 2])
        else:
            x = _swa_layer(x, g[2:3], g[3:4], swa[i // 2])
        x = _ffn_layer(x, g[4:6], w1r[i, 1], w2[i, 1])
    return x


def kernel(x_prompt, x_sample, norm_g, ffn_w1, ffn_w2, gla_w_in, gla_w_gate_f, gla_b_gate_f, gla_w_gate_b,
           gla_b_gate_b, gla_onorm, gla_w_out, swa_w_in, swa_sinks, swa_w_out):
    w1r, w2, gla, swa = _prep_weights(ffn_w1, ffn_w2, gla_w_in, gla_w_gate_f, gla_b_gate_f, gla_w_gate_b,
                                      gla_b_gate_b, gla_onorm, gla_w_out, swa_w_in, swa_sinks, swa_w_out)
    y_prompt = _trunk(x_prompt, norm_g, w1r, w2, gla, swa)
    y_sample = _trunk(x_sample, norm_g, w1r, w2, gla, swa)
    return (y_prompt, y_sample)
```

```python
import functools

import jax
import jax.numpy as jnp
from jax import lax
from jax.experimental import pallas as pl
from jax.experimental.pallas import tpu as pltpu

F32 = jnp.float32
BF16 = jnp.bfloat16

D_MODEL = 1024
DEPTH = 4
NORM_EPS = 1e-6

D_FF = 2816
FFN_RES = 0.5
FFN_CHUNK = 256
FFN_NCHUNK = D_FF // FFN_CHUNK

GLA_HEADS = 4
GLA_DK = 128
GLA_DV = 256
GLA_QK = GLA_HEADS * GLA_DK
GLA_V = GLA_HEADS * GLA_DV
GLA_GATE_RANK = 16
GLA_TAU = 16.0
GLA_C = 128
GLA_MID = GLA_C // 2 - 1

SWA_Q_HEADS = 16
SWA_KV_HEADS = 4
SWA_GROUP = SWA_Q_HEADS // SWA_KV_HEADS
SWA_HD = 64
SWA_WINDOW = 128
SWA_BLOCK = 128
SWA_Q = SWA_Q_HEADS * SWA_HD
SWA_KV = SWA_KV_HEADS * SWA_HD
ROPE_THETA = 500000.0
ROPE_DIM = SWA_HD // 4
ROPE_HALF = ROPE_DIM // 2
NEG_BIG = -1e30
LOG2E = 1.4426950408889634
SWA_QSCALE = SWA_HD ** -0.5 * LOG2E

LANES = 128
SWA_BD = 2 * LANES * SWA_KV_HEADS
SWA_QB = 4
TOKEN_TILE = 512
GLA_REC_TILE = 256
VMEM_LIMIT = 56 * 1024 * 1024


def _rms(x, g):
    ms = jnp.mean(x * x, axis=-1, keepdims=True)
    return x * lax.rsqrt(ms + NORM_EPS) * g


def _silu(x):
    return x * (1.0 / (1.0 + jnp.exp(-x)))


def _const_spec(shape):
    nd = len(shape)
    return pl.BlockSpec(shape, lambda *_: (0,) * nd, pipeline_mode=pl.Buffered(1))


def _params(sem):
    return pltpu.CompilerParams(dimension_semantics=sem, vmem_limit_bytes=VMEM_LIMIT)


def _ffn_body(x_ref, g_ref, w1_ref, w2_ref, o_ref, h_ref):
    x = x_ref[...]
    xn = _rms(x, g_ref[0:1, :]).astype(BF16)
    for c in range(FFN_NCHUNK):
        gu = jnp.dot(xn, w1_ref[:, c * 2 * FFN_CHUNK:(c + 1) * 2 * FFN_CHUNK],
                     preferred_element_type=F32)
        gate = gu[:, :FFN_CHUNK]
        up = gu[:, FFN_CHUNK:]
        h_ref[:, c * FFN_CHUNK:(c + 1) * FFN_CHUNK] = (_silu(gate) * up).astype(BF16)
    y = jnp.dot(h_ref[...], w2_ref[...], preferred_element_type=F32)
    o_ref[...] = x + FFN_RES * _rms(y, g_ref[1:2, :])


def _ffn(x2, g2, w1r, w2):
    t = x2.shape[0]
    tm = TOKEN_TILE
    return pl.pallas_call(
        _ffn_body,
        grid=(t // tm,),
        in_specs=[
            pl.BlockSpec((tm, D_MODEL), lambda i: (i, 0)),
            _const_spec((2, D_MODEL)),
            _const_spec((D_MODEL, 2 * D_FF)),
            _const_spec((D_FF, D_MODEL)),
        ],
        out_specs=pl.BlockSpec((tm, D_MODEL), lambda i: (i, 0)),
        out_shape=jax.ShapeDtypeStruct((t, D_MODEL), F32),
        scratch_shapes=[pltpu.VMEM((tm, D_FF), BF16)],
        compiler_params=_params(("parallel",)),
        name="ffn",
    )(x2, g2, w1r, w2)


def _gla_proj_body(x_ref, g_ref, wm_ref, wgd_ref, wgate_ref, bgate_ref,
                   qkf_ref, qkb_ref, v_ref, r_ref, dl_ref):
    tm = x_ref.shape[0]
    xn = _rms(x_ref[...], g_ref[...]).astype(BF16)
    h = jnp.dot(xn, wm_ref[...], preferred_element_type=F32)
    v_ref[...] = h[:, 2 * GLA_QK:2 * GLA_QK + GLA_V].astype(BF16)
    r_ref[...] = h[:, 2 * GLA_QK + GLA_V:].astype(BF16)
    gd = jnp.dot(xn, wgd_ref[...], preferred_element_type=F32).astype(BF16)
    z = jnp.dot(gd, wgate_ref[...], preferred_element_type=F32) + bgate_ref[...]
    logdecay = (jnp.minimum(z, 0.0) - jnp.log(1.0 + jnp.exp(-jnp.abs(z)))) * (1.0 / GLA_TAU)

    row = lax.broadcasted_iota(jnp.int32, (GLA_C, GLA_C), 0)
    col = lax.broadcasted_iota(jnp.int32, (GLA_C, GLA_C), 1)
    tri_f = (col <= row).astype(BF16)
    tri_b = (col >= row).astype(BF16)

    for c in range(tm // GLA_C):
        rows = slice(c * GLA_C, (c + 1) * GLA_C)
        q = h[rows, :GLA_QK] * (GLA_DK ** -0.5)
        k = h[rows, GLA_QK:2 * GLA_QK]
        for d, (tri, out_ref, last, mid) in enumerate((
                (tri_f, qkf_ref, GLA_C - 1, GLA_MID),
                (tri_b, qkb_ref, 0, GLA_MID + 1))):
            g = logdecay[rows, d * GLA_QK:(d + 1) * GLA_QK]
            g_hi = g.astype(BF16)
            g_lo = (g - g_hi.astype(F32)).astype(BF16)
            b = (jnp.dot(tri, g_hi, preferred_element_type=F32)
                 + jnp.dot(tri, g_lo, preferred_element_type=F32))
            b_mid = b[mid:mid + 1, :]
            b_last = b[last:last + 1, :]
            out_ref[rows, 0 * GLA_QK:1 * GLA_QK] = (q * jnp.exp(b - b_mid)).astype(BF16)
            out_ref[rows, 1 * GLA_QK:2 * GLA_QK] = (q * jnp.exp(b)).astype(BF16)
            out_ref[rows, 2 * GLA_QK:3 * GLA_QK] = (k * jnp.exp(b_mid - b)).astype(BF16)
            out_ref[rows, 3 * GLA_QK:4 * GLA_QK] = (k * jnp.exp(b_last - b)).astype(BF16)
            dl_ref[c, :, d * GLA_QK:(d + 1) * GLA_QK] = jnp.exp(b_last)


def _gla_proj(x2, g_pre, wm, wgd, wgate, bgate):
    t = x2.shape[0]
    tm = TOKEN_TILE
    nc = tm // GLA_C
    tok = lambda w: pl.BlockSpec((tm, w), lambda i: (i, 0))
    return pl.pallas_call(
        _gla_proj_body,
        grid=(t // tm,),
        in_specs=[
            tok(D_MODEL),
            _const_spec((1, D_MODEL)),
            _const_spec((D_MODEL, 2 * GLA_QK + 2 * GLA_V)),
            _const_spec((D_MODEL, LANES)),
            _const_spec((LANES, 2 * GLA_QK)),
            _const_spec((1, 2 * GLA_QK)),
        ],
        out_specs=[
            tok(4 * GLA_QK), tok(4 * GLA_QK), tok(GLA_V), tok(GLA_V),
            pl.BlockSpec((nc, 1, 2 * GLA_QK), lambda i: (i, 0, 0)),
        ],
        out_shape=[
            jax.ShapeDtypeStruct((t, 4 * GLA_QK), BF16),
            jax.ShapeDtypeStruct((t, 4 * GLA_QK), BF16),
            jax.ShapeDtypeStruct((t, GLA_V), BF16),
            jax.ShapeDtypeStruct((t, GLA_V), BF16),
            jax.ShapeDtypeStruct((t // GLA_C, 1, 2 * GLA_QK), F32),
        ],
        compiler_params=_params(("parallel",)),
        name="gla_proj",
    )(x2, g_pre, wm, wgd, wgate, bgate)


def _gla_rec_body(qkf_ref, qkb_ref, vf_ref, vb_ref, dlf_ref, dlb_ref, of_ref, ob_ref, s_ref):
    tb = qkf_ref.shape[1]
    nck = tb // GLA_C

    @pl.when(pl.program_id(1) == 0)
    def _():
        s_ref[...] = jnp.zeros_like(s_ref)

    row = lax.broadcasted_iota(jnp.int32, (GLA_C, GLA_C), 0)
    col = lax.broadcasted_iota(jnp.int32, (GLA_C, GLA_C), 1)

    def unit(qk_ref, v_ref, dl_ref, o_ref, mask, c, head, d):
        rows = slice(c * GLA_C, (c + 1) * GLA_C)
        lanes = lambda part: slice(part * GLA_QK + head * GLA_DK, part * GLA_QK + (head + 1) * GLA_DK)
        q_mid = qk_ref[0, rows, lanes(0)]
        q_dec = qk_ref[0, rows, lanes(1)]
        k_mid = qk_ref[0, rows, lanes(2)]
        k_end = qk_ref[0, rows, lanes(3)]
        vv = v_ref[0, rows, head * GLA_DV:(head + 1) * GLA_DV]
        state = s_ref[d * GLA_HEADS + head]
        scores = lax.dot_general(q_mid, k_mid, (((1,), (1,)), ((), ())), preferred_element_type=F32)
        a = jnp.where(mask, scores, 0.0).astype(BF16)
        lhs = jnp.concatenate([a, q_dec], axis=1)
        rhs = jnp.concatenate([vv, state.astype(BF16)], axis=0)
        o_ref[0, rows, head * GLA_DV:(head + 1) * GLA_DV] = jnp.dot(lhs, rhs, preferred_element_type=F32)
        kv = lax.dot_general(k_end, vv, (((0,), (0,)), ((), ())), preferred_element_type=F32)
        dl_row = dl_ref[0, c, :, d * GLA_QK + head * GLA_DK:d * GLA_QK + (head + 1) * GLA_DK]
        dl_col = jnp.transpose(jnp.broadcast_to(dl_row, (GLA_DK, GLA_DK)))
        s_ref[d * GLA_HEADS + head] = jnp.concatenate([dl_col, dl_col], axis=1) * state + kv

    for c in range(nck):
        for head in range(GLA_HEADS):
            unit(qkf_ref, vf_ref, dlf_ref, of_ref, col <= row, c, head, 0)
    for c in reversed(range(nck)):
        for head in range(GLA_HEADS):
            unit(qkb_ref, vb_ref, dlb_ref, ob_ref, col >= row, c, head, 1)


def _gla_rec(qkf, qkb, v, dl, bsz, seq):
    tb = GLA_REC_TILE
    nb = seq // tb
    nck = tb // GLA_C
    fwd = lambda b, i: (b, i, 0)
    bwd = lambda b, i: (b, nb - 1 - i, 0)
    return pl.pallas_call(
        _gla_rec_body,
        grid=(bsz, nb),
        in_specs=[
            pl.BlockSpec((1, tb, 4 * GLA_QK), fwd),
            pl.BlockSpec((1, tb, 4 * GLA_QK), bwd),
            pl.BlockSpec((1, tb, GLA_V), fwd),
            pl.BlockSpec((1, tb, GLA_V), bwd),
            pl.BlockSpec((1, nck, 1, 2 * GLA_QK), lambda b, i: (b, i, 0, 0)),
            pl.BlockSpec((1, nck, 1, 2 * GLA_QK), lambda b, i: (b, nb - 1 - i, 0, 0)),
        ],
        out_specs=[
            pl.BlockSpec((1, tb, GLA_V), fwd),
            pl.BlockSpec((1, tb, GLA_V), bwd),
        ],
        out_shape=[
            jax.ShapeDtypeStruct((bsz, seq, GLA_V), F32),
            jax.ShapeDtypeStruct((bsz, seq, GLA_V), F32),
        ],
        scratch_shapes=[pltpu.VMEM((2 * GLA_HEADS, GLA_DK, GLA_DV), F32)],
        compiler_params=_params(("parallel", "arbitrary")),
        name="gla_rec",
    )(qkf, qkb, v, v, dl, dl)


def _gla_out_body(of_ref, ob_ref, r_ref, x_ref, gon_ref, gpost_ref, wout_ref, y_ref):
    o = of_ref[...] + ob_ref[...]
    parts = []
    for head in range(GLA_HEADS):
        oh = o[:, head * GLA_DV:(head + 1) * GLA_DV]
        ms = jnp.mean(oh * oh, axis=-1, keepdims=True)
        parts.append(oh * lax.rsqrt(ms + NORM_EPS))
    on = jnp.concatenate(parts, axis=1) * gon_ref[...]
    gated = (on * _silu(r_ref[...].astype(F32))).astype(BF16)
    y = jnp.dot(gated, wout_ref[...], preferred_element_type=F32)
    y_ref[...] = x_ref[...] + _rms(y, gpost_ref[...])


def _gla_out(o_f, o_b, r, x2, g_onorm, g_post, w_out):
    t = x2.shape[0]
    tm = TOKEN_TILE
    tok = lambda: pl.BlockSpec((tm, D_MODEL), lambda i: (i, 0))
    return pl.pallas_call(
        _gla_out_body,
        grid=(t // tm,),
        in_specs=[tok(), tok(), tok(), tok(),
                  _const_spec((1, GLA_V)), _const_spec((1, D_MODEL)), _const_spec((GLA_V, D_MODEL))],
        out_specs=tok(),
        out_shape=jax.ShapeDtypeStruct((t, D_MODEL), F32),
        compiler_params=_params(("parallel",)),
        name="gla_out",
    )(o_f, o_b, r, x2, g_onorm, g_post, w_out)


def _gla_layer(x, g_pre, g_post, w):
    bsz, seq, _ = x.shape
    x2 = x.reshape(bsz * seq, D_MODEL)
    qkf, qkb, v, r, dl = _gla_proj(x2, g_pre, w["wm"], w["wgd"], w["wgate"], w["bgate"])
    shp = lambda a: a.reshape(bsz, seq, a.shape[-1])
    o_f, o_b = _gla_rec(shp(qkf), shp(qkb), shp(v), dl.reshape(bsz, seq // GLA_C, 1, 2 * GLA_QK), bsz, seq)
    y = _gla_out(o_f.reshape(bsz * seq, GLA_V), o_b.reshape(bsz * seq, GLA_V), r, x2,
                 w["onorm"], g_post, w["wout"])
    return y.reshape(bsz, seq, D_MODEL)


def _rope(z, cos, sin_a, sin_b):
    return (z * cos + pltpu.roll(z, LANES - ROPE_HALF, axis=1) * sin_a
            + pltpu.roll(z, ROPE_HALF, axis=1) * sin_b)


def _swa_proj_body(x_ref, g_ref, w_ref, cos_ref, sa_ref, sb_ref, q_ref, kbd_ref, vbd_ref):
    xn = _rms(x_ref[0], g_ref[...]).astype(BF16)
    h = jnp.dot(xn, w_ref[...], preferred_element_type=F32)
    cos, sin_a, sin_b = cos_ref[...], sa_ref[...], sb_ref[...]
    for j in range(SWA_Q // LANES):
        blk = slice(j * LANES, (j + 1) * LANES)
        q_ref[0, :, blk] = (_rope(h[:, blk], cos, sin_a, sin_b) * SWA_QSCALE).astype(BF16)
    low = lax.broadcasted_iota(jnp.int32, (x_ref.shape[1], LANES), 1) < SWA_HD

    def spread(z, out_ref, j):
        zr = pltpu.roll(z, SWA_HD, axis=1)
        parts = (jnp.where(low, z, 0.0), jnp.where(low, 0.0, zr), jnp.where(low, zr, 0.0), jnp.where(low, 0.0, z))
        for i, part in enumerate(parts):
            out_ref[0, :, (4 * j + i) * LANES:(4 * j + i + 1) * LANES] = part.astype(BF16)

    for j in range(SWA_KV // LANES):
        kcols = slice(SWA_Q + j * LANES, SWA_Q + (j + 1) * LANES)
        vcols = slice(SWA_Q + SWA_KV + j * LANES, SWA_Q + SWA_KV + (j + 1) * LANES)
        spread(_rope(h[:, kcols], cos, sin_a, sin_b), kbd_ref, j)
        spread(h[:, vcols], vbd_ref, j)


def _swa_proj(x, g_pre, w_in, cos, sin_a, sin_b):
    bsz, seq, _ = x.shape
    tm = TOKEN_TILE
    tok = lambda w: pl.BlockSpec((1, tm, w), lambda b, i: (b, i, 0))
    tab = lambda: pl.BlockSpec((tm, LANES), lambda b, i: (i, 0))
    return pl.pallas_call(
        _swa_proj_body,
        grid=(bsz, seq // tm),
        in_specs=[tok(D_MODEL), _const_spec((1, D_MODEL)), _const_spec((D_MODEL, SWA_Q + 2 * SWA_KV)),
                  tab(), tab(), tab()],
        out_specs=[tok(SWA_Q), tok(SWA_BD), tok(SWA_BD)],
        out_shape=[
            jax.ShapeDtypeStruct((bsz, seq, SWA_Q), BF16),
            jax.ShapeDtypeStruct((bsz, seq, SWA_BD), BF16),
            jax.ShapeDtypeStruct((bsz, seq, SWA_BD), BF16),
        ],
        compiler_params=_params(("parallel", "parallel")),
        name="swa_proj",
    )(x, g_pre, w_in, cos, sin_a, sin_b)


def _swa_attn_body(nq, sink_ref, q_ref, kp_ref, km_ref, kn_ref, vp_ref, vm_ref, vn_ref,
                   x_ref, gpost_ref, wout_ref, y_ref, s_ref, p_ref, o_ref, inv_ref):
    qb = SWA_QB
    n = pl.program_id(1)
    blk = SWA_BLOCK
    row = lax.broadcasted_iota(jnp.int32, (blk, 2 * LANES), 0)
    col = lax.broadcasted_iota(jnp.int32, (blk, 2 * LANES), 1) & (LANES - 1)
    tri_prev = col >= row
    tri_next = col <= row
    head_a = lax.broadcasted_iota(jnp.int32, (blk, 2 * LANES), 1) < LANES
    low = lax.broadcasted_iota(jnp.int32, (blk, LANES), 1) < SWA_HD

    def key_block(prev_ref, main_ref, next_ref, j, lanes):
        if j == 0:
            return prev_ref[0, :, lanes]
        if j == qb + 1:
            return next_ref[0, :, lanes]
        return main_ref[0, (j - 1) * blk:j * blk, lanes]

    for hk in range(SWA_KV_HEADS):
        bd = slice(2 * hk * LANES, (2 * hk + 2) * LANES)
        pair_lanes = [slice((2 * hk + e) * LANES, (2 * hk + e + 1) * LANES) for e in range(2)]
        for j in range(qb + 2):
            i0, i1 = max(j - 2, 0), min(j, qb - 1)
            kj = key_block(kp_ref, km_ref, kn_ref, j, bd)
            rhs = jnp.concatenate([kj[:, :LANES], kj[:, LANES:]], axis=0)
            lhs = jnp.concatenate([q_ref[0, i0 * blk:(i1 + 1) * blk, pl_] for pl_ in pair_lanes], axis=0)
            s = lax.dot_general(lhs, rhs, (((1,), (1,)), ((), ())), preferred_element_type=F32)
            nrow = (i1 - i0 + 1) * blk
            for e in range(2):
                for i in range(i0, i1 + 1):
                    r0 = e * nrow + (i - i0) * blk
                    s_ref[e, i, j - i] = s[r0:r0 + blk, :]
        for e in range(2):
            sink_a = sink_ref[4 * hk + 2 * e] * LOG2E
            sink_b = sink_ref[4 * hk + 2 * e + 1] * LOG2E
            for i in range(qb):
                g = n * qb + i
                sp = jnp.where(jnp.logical_and(tri_prev, g > 0), s_ref[e, i, 0], NEG_BIG)
                sc = s_ref[e, i, 1]
                sn = jnp.where(jnp.logical_and(tri_next, g < nq - 1), s_ref[e, i, 2], NEG_BIG)
                mx = jnp.maximum(jnp.maximum(sp, sc), sn)
                m_a = jnp.maximum(jnp.max(mx[:, :LANES], axis=-1, keepdims=True), sink_a)
                m_b = jnp.maximum(jnp.max(mx[:, LANES:], axis=-1, keepdims=True), sink_b)
                m = jnp.where(head_a, m_a, m_b)
                pp, pc, pn = jnp.exp2(sp - m), jnp.exp2(sc - m), jnp.exp2(sn - m)
                tot = pp + pc + pn
                d_a = jnp.sum(tot[:, :LANES], axis=-1, keepdims=True)
                d_b = jnp.sum(tot[:, LANES:], axis=-1, keepdims=True)
                inv_a = 1.0 / (d_a + jnp.exp2(sink_a - m_a))
                inv_b = 1.0 / (d_b + jnp.exp2(sink_b - m_b))
                inv_ref[i * blk:(i + 1) * blk, pair_lanes[e]] = jnp.where(low, inv_a, inv_b)
                for which, pw in enumerate((pp, pc, pn)):
                    slot = 2 - which
                    p_ref[e, i + which, slot * blk:(slot + 1) * blk, :] = pw.astype(BF16)
        for j in range(qb + 2):
            i0, i1 = max(j - 2, 0), min(j, qb - 1)
            vj = key_block(vp_ref, vm_ref, vn_ref, j, bd)
            rhs = jnp.concatenate([vj[:, :LANES], vj[:, LANES:]], axis=0)
            r0, r1 = (i0 - (j - 2)) * blk, (i1 - (j - 2) + 1) * blk
            lhs = jnp.concatenate([p_ref[e, j, r0:r1, :] for e in range(2)], axis=0)
            o = jnp.dot(lhs, rhs, preferred_element_type=F32)
            nrow = r1 - r0
            for e in range(2):
                for i in range(i0, i1 + 1):
                    part = o[e * nrow + (i - i0) * blk:e * nrow + (i - i0 + 1) * blk, :]
                    rows = slice(i * blk, (i + 1) * blk)
                    if j == i:
                        o_ref[rows, pair_lanes[e]] = part
                    else:
                        o_ref[rows, pair_lanes[e]] += part
    o_all = (o_ref[...] * inv_ref[...]).astype(BF16)
    y = jnp.dot(o_all, wout_ref[...], preferred_element_type=F32)
    y_ref[0] = x_ref[0] + _rms(y, gpost_ref[...])


def _swa_attn(q, k, v, x, sinks, g_post, w_out):
    bsz, seq, _ = x.shape
    nq = seq // SWA_BLOCK
    qb = SWA_QB
    tq = qb * SWA_BLOCK
    cur = lambda b, n: (b, n, 0)
    prev = lambda b, n: (b, jnp.maximum(n * qb - 1, 0), 0)
    nxt = lambda b, n: (b, jnp.minimum((n + 1) * qb, nq - 1), 0)
    halo = lambda im: pl.BlockSpec((1, SWA_BLOCK, SWA_BD), im)
    main = lambda w: pl.BlockSpec((1, tq, w), cur)
    return pl.pallas_call(
        functools.partial(_swa_attn_body, nq),
        grid=(bsz, nq // qb),
        in_specs=[
            pl.BlockSpec(memory_space=pltpu.SMEM),
            main(SWA_Q),
            halo(prev), main(SWA_BD), halo(nxt), halo(prev), main(SWA_BD), halo(nxt),
            main(D_MODEL),
            _const_spec((1, D_MODEL)),
            _const_spec((SWA_Q, D_MODEL)),
        ],
        out_specs=main(D_MODEL),
        out_shape=jax.ShapeDtypeStruct((bsz, seq, D_MODEL), F32),
        scratch_shapes=[
            pltpu.VMEM((2, qb, 3, SWA_BLOCK, 2 * LANES), F32),
            pltpu.VMEM((2, qb + 2, 3 * SWA_BLOCK, 2 * LANES), BF16),
            pltpu.VMEM((tq, SWA_Q), F32),
            pltpu.VMEM((tq, SWA_Q), F32),
        ],
        compiler_params=_params(("parallel", "parallel")),
        name="swa_attn",
    )(sinks, q, k, k, k, v, v, v, x, g_post, w_out)


def _rope_tables(seq):
    inv_freq = ROPE_THETA ** (-(jnp.arange(ROPE_HALF, dtype=F32) * 2.0 / ROPE_DIM))
    ang = jnp.arange(seq, dtype=F32)[:, None] * inv_freq[None, :]
    cos8, sin8 = jnp.cos(ang), jnp.sin(ang)
    ones = jnp.ones((seq, SWA_HD - ROPE_DIM), F32)
    zeros8 = jnp.zeros((seq, ROPE_HALF), F32)
    zeros = jnp.zeros((seq, SWA_HD - ROPE_DIM), F32)
    cos = jnp.concatenate([cos8, cos8, ones], axis=1)
    sin_a = jnp.concatenate([-sin8, zeros8, zeros], axis=1)
    sin_b = jnp.concatenate([zeros8, sin8, zeros], axis=1)
    rep = LANES // SWA_HD
    return tuple(jnp.tile(t, (1, rep)) for t in (cos, sin_a, sin_b))


def _swa_layer(x, g_pre, g_post, w):
    seq = x.shape[1]
    cos, sin_a, sin_b = _rope_tables(seq)
    q, k, v = _swa_proj(x, g_pre, w["win"], cos, sin_a, sin_b)
    return _swa_attn(q, k, v, x, w["sinks"], g_post, w["wout"])


def _ffn_layer(x, g2, w1r, w2):
    bsz, seq, _ = x.shape
    return _ffn(x.reshape(bsz * seq, D_MODEL), g2, w1r, w2).reshape(bsz, seq, D_MODEL)


def _prep_weights(ffn_w1, ffn_w2, gla_w_in, gla_w_gate_f, gla_b_gate_f, gla_w_gate_b, gla_b_gate_b,
                  gla_onorm, gla_w_out, swa_w_in, swa_sinks, swa_w_out):
    w1r = ffn_w1.reshape(DEPTH, 2, D_MODEL, 2, FFN_NCHUNK, FFN_CHUNK)
    w1r = jnp.transpose(w1r, (0, 1, 2, 4, 3, 5)).reshape(DEPTH, 2, D_MODEL, 2 * D_FF).astype(BF16)
    w2 = ffn_w2.astype(BF16)
    gla = []
    for j in range(gla_w_in.shape[0]):
        n_main = 2 * GLA_QK + 2 * GLA_V
        wgd = jnp.zeros((D_MODEL, LANES), F32).at[:, :2 * GLA_GATE_RANK].set(gla_w_in[j][:, n_main:])
        wgate = jnp.zeros((LANES, 2 * GLA_QK), F32)
        wgate = wgate.at[:GLA_GATE_RANK, :GLA_QK].set(gla_w_gate_f[j])
        wgate = wgate.at[GLA_GATE_RANK:2 * GLA_GATE_RANK, GLA_QK:].set(gla_w_gate_b[j])
        gla.append(dict(
            wm=gla_w_in[j][:, :n_main].astype(BF16),
            wgd=wgd.astype(BF16),
            wgate=wgate.astype(BF16),
            bgate=jnp.concatenate([gla_b_gate_f[j], gla_b_gate_b[j]])[None, :],
            onorm=gla_onorm[j][None, :],
            wout=gla_w_out[j].astype(BF16),
        ))
    swa = []
    for j in range(swa_w_in.shape[0]):
        swa.append(dict(win=swa_w_in[j].astype(BF16), sinks=swa_sinks[j], wout=swa_w_out[j].astype(BF16)))
    return w1r, w2, gla, swa


def _trunk(x, norm_g, w1r, w2, gla, swa):
    for i in range(DEPTH):
        g = norm_g[i]
        x = _ffn_layer(x, g[0:2], w1r[i, 0], w2[i, 0])
        if i % 2 == 0:
            x = _gla_layer(x, g[2:3], g[3:4], gla[i // 2])
        else:
            x = _swa_layer(x, g[2:3], g[3:4], swa[i // 2])
        x = _ffn_layer(x, g[4:6], w1r[i, 1], w2[i, 1])
    return x


def kernel(x_prompt, x_sample, norm_g, ffn_w1, ffn_w2, gla_w_in, gla_w_gate_f, gla_b_gate_f, gla_w_gate_b,
           gla_b_gate_b, gla_onorm, gla_w_out, swa_w_in, swa_sinks, swa_w_out):
    w1r, w2, gla, swa = _prep_weights(ffn_w1, ffn_w2, gla_w_in, gla_w_gate_f, gla_b_gate_f, gla_w_gate_b,
                                      gla_b_gate_b, gla_onorm, gla_w_out, swa_w_in, swa_sinks, swa_w_out)
    y_prompt = _trunk(x_prompt, norm_g, w1r, w2, gla, swa)
    y_sample = _trunk(x_sample, norm_g, w1r, w2, gla, swa)
    return (y_prompt, y_sample)
```

```python
import functools

import jax
import jax.numpy as jnp
from jax import lax
from jax.experimental import pallas as pl
from jax.experimental.pallas import tpu as pltpu

F32 = jnp.float32
BF16 = jnp.bfloat16

D_MODEL = 1024
DEPTH = 4
NORM_EPS = 1e-6

D_FF = 2816
FFN_RES = 0.5
FFN_CHUNK = 256
FFN_NCHUNK = D_FF // FFN_CHUNK

GLA_HEADS = 4
GLA_DK = 128
GLA_DV = 256
GLA_QK = GLA_HEADS * GLA_DK
GLA_V = GLA_HEADS * GLA_DV
GLA_GATE_RANK = 16
GLA_TAU = 16.0
GLA_C = 128
GLA_MID = GLA_C // 2 - 1

SWA_Q_HEADS = 16
SWA_KV_HEADS = 4
SWA_GROUP = SWA_Q_HEADS // SWA_KV_HEADS
SWA_HD = 64
SWA_WINDOW = 128
SWA_BLOCK = 128
SWA_Q = SWA_Q_HEADS * SWA_HD
SWA_KV = SWA_KV_HEADS * SWA_HD
ROPE_THETA = 500000.0
ROPE_DIM = SWA_HD // 4
ROPE_HALF = ROPE_DIM // 2
NEG_BIG = -1e30
LOG2E = 1.4426950408889634
SWA_QSCALE = SWA_HD ** -0.5 * LOG2E

LANES = 128
SWA_BD = 2 * LANES * SWA_KV_HEADS
SWA_QB = 4
TOKEN_TILE = 512
GLA_PROJ_TILE = 1024
GLA_PROJ_SUB = 256
GLA_REC_TILE = 512
VMEM_LIMIT = 56 * 1024 * 1024


def _rms(x, g):
    ms = jnp.mean(x * x, axis=-1, keepdims=True)
    return x * lax.rsqrt(ms + NORM_EPS) * g


def _silu(x):
    return x * (1.0 / (1.0 + jnp.exp(-x)))


def _const_spec(shape):
    nd = len(shape)
    return pl.BlockSpec(shape, lambda *_: (0,) * nd, pipeline_mode=pl.Buffered(1))


def _params(sem):
    return pltpu.CompilerParams(dimension_semantics=sem, vmem_limit_bytes=VMEM_LIMIT)


def _ffn_body(x_ref, g_ref, w1_ref, w2_ref, o_ref, h_ref):
    x = x_ref[...]
    xn = _rms(x, g_ref[0:1, :]).astype(BF16)
    for c in range(FFN_NCHUNK):
        cols = slice(c * FFN_CHUNK, (c + 1) * FFN_CHUNK)
        gate = jnp.dot(xn, w1_ref[:, cols], preferred_element_type=F32)
        up = jnp.dot(xn, w1_ref[:, D_FF + c * FFN_CHUNK:D_FF + (c + 1) * FFN_CHUNK], preferred_element_type=F32)
        h_ref[:, cols] = (_silu(gate) * up).astype(BF16)
    y = jnp.dot(h_ref[...], w2_ref[...], preferred_element_type=F32)
    o_ref[...] = x + FFN_RES * _rms(y, g_ref[1:2, :])


def _ffn(x2, g2, w1r, w2):
    t = x2.shape[0]
    tm = TOKEN_TILE
    return pl.pallas_call(
        _ffn_body,
        grid=(t // tm,),
        in_specs=[
            pl.BlockSpec((tm, D_MODEL), lambda i: (i, 0)),
            _const_spec((2, D_MODEL)),
            _const_spec((D_MODEL, 2 * D_FF)),
            _const_spec((D_FF, D_MODEL)),
        ],
        out_specs=pl.BlockSpec((tm, D_MODEL), lambda i: (i, 0)),
        out_shape=jax.ShapeDtypeStruct((t, D_MODEL), F32),
        scratch_shapes=[pltpu.VMEM((tm, D_FF), BF16)],
        compiler_params=_params(("parallel",)),
        name="ffn",
    )(x2, g2, w1r, w2)


def _gla_proj_body(x_ref, g_ref, wm_ref, wgd_ref, wgate_ref, bgate_ref,
                   qkf_ref, qkb_ref, v_ref, r_ref, dl_ref):
    tm = x_ref.shape[0]
    sub = GLA_PROJ_SUB
    row = lax.broadcasted_iota(jnp.int32, (GLA_C, GLA_C), 0)
    col = lax.broadcasted_iota(jnp.int32, (GLA_C, GLA_C), 1)
    tri_f = (col <= row).astype(BF16)
    tri_b = (col >= row).astype(BF16)

    def project(s):
        rows = slice(s * sub, (s + 1) * sub)
        xn = _rms(x_ref[rows, :], g_ref[...]).astype(BF16)
        h = jnp.dot(xn, wm_ref[...], preferred_element_type=F32)
        gd = jnp.dot(xn, wgd_ref[...], preferred_element_type=F32).astype(BF16)
        z = jnp.dot(gd, wgate_ref[...], preferred_element_type=F32) + bgate_ref[...]
        return h, z

    def finish(s, h, z):
        rows = slice(s * sub, (s + 1) * sub)
        v_ref[rows, :] = h[:, 2 * GLA_QK:2 * GLA_QK + GLA_V].astype(BF16)
        r_ref[rows, :] = h[:, 2 * GLA_QK + GLA_V:].astype(BF16)
        logdecay = (jnp.minimum(z, 0.0) - jnp.log(1.0 + jnp.exp(-jnp.abs(z)))) * (LOG2E / GLA_TAU)
        for cc in range(sub // GLA_C):
            c = s * (sub // GLA_C) + cc
            crow = slice(cc * GLA_C, (cc + 1) * GLA_C)
            orow = slice(c * GLA_C, (c + 1) * GLA_C)
            q = h[crow, :GLA_QK] * (GLA_DK ** -0.5)
            k = h[crow, GLA_QK:2 * GLA_QK]
            for d, (tri, out_ref, last, mid) in enumerate((
                    (tri_f, qkf_ref, GLA_C - 1, GLA_MID),
                    (tri_b, qkb_ref, 0, GLA_MID + 1))):
                g = logdecay[crow, d * GLA_QK:(d + 1) * GLA_QK]
                g_hi = g.astype(BF16)
                g_lo = (g - g_hi.astype(F32)).astype(BF16)
                b = (jnp.dot(tri, g_hi, preferred_element_type=F32)
                     + jnp.dot(tri, g_lo, preferred_element_type=F32))
                b_mid = b[mid:mid + 1, :]
                b_last = b[last:last + 1, :]
                out_ref[orow, 0 * GLA_QK:1 * GLA_QK] = (q * jnp.exp2(b - b_mid)).astype(BF16)
                out_ref[orow, 1 * GLA_QK:2 * GLA_QK] = (q * jnp.exp2(b)).astype(BF16)
                out_ref[orow, 2 * GLA_QK:3 * GLA_QK] = (k * jnp.exp2(b_mid - b)).astype(BF16)
                out_ref[orow, 3 * GLA_QK:4 * GLA_QK] = (k * jnp.exp2(b_last - b)).astype(BF16)
                dl_ref[c, :, d * GLA_QK:(d + 1) * GLA_QK] = jnp.exp2(b_last)

    pending = project(0)
    for s in range(tm // sub):
        nxt = project(s + 1) if s + 1 < tm // sub else None
        finish(s, *pending)
        pending = nxt


def _gla_proj(x2, g_pre, wm, wgd, wgate, bgate):
    t = x2.shape[0]
    tm = GLA_PROJ_TILE
    nc = tm // GLA_C
    tok = lambda w: pl.BlockSpec((tm, w), lambda i: (i, 0))
    return pl.pallas_call(
        _gla_proj_body,
        grid=(t // tm,),
        in_specs=[
            tok(D_MODEL),
            _const_spec((1, D_MODEL)),
            _const_spec((D_MODEL, 2 * GLA_QK + 2 * GLA_V)),
            _const_spec((D_MODEL, LANES)),
            _const_spec((LANES, 2 * GLA_QK)),
            _const_spec((1, 2 * GLA_QK)),
        ],
        out_specs=[
            tok(4 * GLA_QK), tok(4 * GLA_QK), tok(GLA_V), tok(GLA_V),
            pl.BlockSpec((nc, 1, 2 * GLA_QK), lambda i: (i, 0, 0)),
        ],
        out_shape=[
            jax.ShapeDtypeStruct((t, 4 * GLA_QK), BF16),
            jax.ShapeDtypeStruct((t, 4 * GLA_QK), BF16),
            jax.ShapeDtypeStruct((t, GLA_V), BF16),
            jax.ShapeDtypeStruct((t, GLA_V), BF16),
            jax.ShapeDtypeStruct((t // GLA_C, 1, 2 * GLA_QK), F32),
        ],
        compiler_params=_params(("parallel",)),
        name="gla_proj",
    )(x2, g_pre, wm, wgd, wgate, bgate)


def _gla_rec_body(qkf_ref, qkb_ref, vf_ref, vb_ref, dlf_ref, dlb_ref, of_ref, ob_ref, s_ref, a_ref, kv_ref):
    tb = qkf_ref.shape[1]
    nck = tb // GLA_C

    @pl.when(pl.program_id(1) == 0)
    def _():
        s_ref[...] = jnp.zeros_like(s_ref)

    row = lax.broadcasted_iota(jnp.int32, (GLA_C, GLA_C), 0)
    col = lax.broadcasted_iota(jnp.int32, (GLA_C, GLA_C), 1)
    dirs = ((qkf_ref, vf_ref, dlf_ref, of_ref, col <= row), (qkb_ref, vb_ref, dlb_ref, ob_ref, col >= row))
    lanes = lambda part, head: slice(part * GLA_QK + head * GLA_DK, part * GLA_QK + (head + 1) * GLA_DK)
    vlanes = lambda head: slice(head * GLA_DV, (head + 1) * GLA_DV)
    unit = lambda d, c, head: (d * nck + c) * GLA_HEADS + head

    for d, (qk_ref, v_ref, _, _, mask) in enumerate(dirs):
        for c in range(nck):
            rows = slice(c * GLA_C, (c + 1) * GLA_C)
            for head in range(GLA_HEADS):
                scores = lax.dot_general(qk_ref[0, rows, lanes(0, head)], qk_ref[0, rows, lanes(2, head)],
                                         (((1,), (1,)), ((), ())), preferred_element_type=F32)
                a_ref[unit(d, c, head)] = jnp.where(mask, scores, 0.0).astype(BF16)
                kv_ref[unit(d, c, head)] = lax.dot_general(
                    qk_ref[0, rows, lanes(3, head)], v_ref[0, rows, vlanes(head)],
                    (((0,), (0,)), ((), ())), preferred_element_type=F32)

    for step in range(nck):
        for d, (qk_ref, v_ref, dl_ref, o_ref, _) in enumerate(dirs):
            c = step if d == 0 else nck - 1 - step
            rows = slice(c * GLA_C, (c + 1) * GLA_C)
            for head in range(GLA_HEADS):
                state = s_ref[d * GLA_HEADS + head]
                lhs = jnp.concatenate([a_ref[unit(d, c, head)], qk_ref[0, rows, lanes(1, head)]], axis=1)
                rhs = jnp.concatenate([v_ref[0, rows, vlanes(head)], state.astype(BF16)], axis=0)
                o_ref[0, rows, vlanes(head)] = jnp.dot(lhs, rhs, preferred_element_type=F32)
                dl_row = dl_ref[0, c, :, lanes(d, head)]
                dl_col = jnp.transpose(jnp.broadcast_to(dl_row, (GLA_DK, GLA_DK)))
                s_ref[d * GLA_HEADS + head] = (jnp.concatenate([dl_col, dl_col], axis=1) * state
                                               + kv_ref[unit(d, c, head)])


def _gla_rec(qkf, qkb, v, dl, bsz, seq):
    tb = GLA_REC_TILE
    nb = seq // tb
    nck = tb // GLA_C
    fwd = lambda b, i: (b, i, 0)
    bwd = lambda b, i: (b, nb - 1 - i, 0)
    return pl.pallas_call(
        _gla_rec_body,
        grid=(bsz, nb),
        in_specs=[
            pl.BlockSpec((1, tb, 4 * GLA_QK), fwd),
            pl.BlockSpec((1, tb, 4 * GLA_QK), bwd),
            pl.BlockSpec((1, tb, GLA_V), fwd),
            pl.BlockSpec((1, tb, GLA_V), bwd),
            pl.BlockSpec((1, nck, 1, 2 * GLA_QK), lambda b, i: (b, i, 0, 0)),
            pl.BlockSpec((1, nck, 1, 2 * GLA_QK), lambda b, i: (b, nb - 1 - i, 0, 0)),
        ],
        out_specs=[
            pl.BlockSpec((1, tb, GLA_V), fwd),
            pl.BlockSpec((1, tb, GLA_V), bwd),
        ],
        out_shape=[
            jax.ShapeDtypeStruct((bsz, seq, GLA_V), F32),
            jax.ShapeDtypeStruct((bsz, seq, GLA_V), F32),
        ],
        scratch_shapes=[
            pltpu.VMEM((2 * GLA_HEADS, GLA_DK, GLA_DV), F32),
            pltpu.VMEM((2 * nck * GLA_HEADS, GLA_C, GLA_C), BF16),
            pltpu.VMEM((2 * nck * GLA_HEADS, GLA_DK, GLA_DV), F32),
        ],
        compiler_params=_params(("parallel", "arbitrary")),
        name="gla_rec",
    )(qkf, qkb, v, v, dl, dl)


def _gla_out_body(of_ref, ob_ref, r_ref, x_ref, gon_ref, gpost_ref, wout_ref, y_ref):
    o = of_ref[...] + ob_ref[...]
    parts = []
    for head in range(GLA_HEADS):
        oh = o[:, head * GLA_DV:(head + 1) * GLA_DV]
        ms = jnp.mean(oh * oh, axis=-1, keepdims=True)
        parts.append(oh * lax.rsqrt(ms + NORM_EPS))
    on = jnp.concatenate(parts, axis=1) * gon_ref[...]
    gated = (on * _silu(r_ref[...].astype(F32))).astype(BF16)
    y = jnp.dot(gated, wout_ref[...], preferred_element_type=F32)
    y_ref[...] = x_ref[...] + _rms(y, gpost_ref[...])


def _gla_out(o_f, o_b, r, x2, g_onorm, g_post, w_out):
    t = x2.shape[0]
    tm = TOKEN_TILE
    tok = lambda: pl.BlockSpec((tm, D_MODEL), lambda i: (i, 0))
    return pl.pallas_call(
        _gla_out_body,
        grid=(t // tm,),
        in_specs=[tok(), tok(), tok(), tok(),
                  _const_spec((1, GLA_V)), _const_spec((1, D_MODEL)), _const_spec((GLA_V, D_MODEL))],
        out_specs=tok(),
        out_shape=jax.ShapeDtypeStruct((t, D_MODEL), F32),
        compiler_params=_params(("parallel",)),
        name="gla_out",
    )(o_f, o_b, r, x2, g_onorm, g_post, w_out)


def _gla_layer(x, g_pre, g_post, w):
    bsz, seq, _ = x.shape
    x2 = x.reshape(bsz * seq, D_MODEL)
    qkf, qkb, v, r, dl = _gla_proj(x2, g_pre, w["wm"], w["wgd"], w["wgate"], w["bgate"])
    shp = lambda a: a.reshape(bsz, seq, a.shape[-1])
    o_f, o_b = _gla_rec(shp(qkf), shp(qkb), shp(v), dl.reshape(bsz, seq // GLA_C, 1, 2 * GLA_QK), bsz, seq)
    y = _gla_out(o_f.reshape(bsz * seq, GLA_V), o_b.reshape(bsz * seq, GLA_V), r, x2,
                 w["onorm"], g_post, w["wout"])
    return y.reshape(bsz, seq, D_MODEL)


def _rope(z, cos, sin_a, sin_b):
    return (z * cos + pltpu.roll(z, LANES - ROPE_HALF, axis=1) * sin_a
            + pltpu.roll(z, ROPE_HALF, axis=1) * sin_b)


def _swa_proj_body(x_ref, g_ref, w_ref, cos_ref, sa_ref, sb_ref, q_ref, kbd_ref, vbd_ref):
    xn = _rms(x_ref[0], g_ref[...]).astype(BF16)
    h = jnp.dot(xn, w_ref[...], preferred_element_type=F32)
    cos, sin_a, sin_b = cos_ref[...], sa_ref[...], sb_ref[...]
    for j in range(SWA_Q // LANES):
        blk = slice(j * LANES, (j + 1) * LANES)
        q_ref[0, :, blk] = (_rope(h[:, blk], cos, sin_a, sin_b) * SWA_QSCALE).astype(BF16)
    low = lax.broadcasted_iota(jnp.int32, (x_ref.shape[1], LANES), 1) < SWA_HD

    def spread(z, out_ref, j):
        zr = pltpu.roll(z, SWA_HD, axis=1)
        parts = (jnp.where(low, z, 0.0), jnp.where(low, 0.0, zr), jnp.where(low, zr, 0.0), jnp.where(low, 0.0, z))
        for i, part in enumerate(parts):
            out_ref[0, :, (4 * j + i) * LANES:(4 * j + i + 1) * LANES] = part.astype(BF16)

    for j in range(SWA_KV // LANES):
        kcols = slice(SWA_Q + j * LANES, SWA_Q + (j + 1) * LANES)
        vcols = slice(SWA_Q + SWA_KV + j * LANES, SWA_Q + SWA_KV + (j + 1) * LANES)
        spread(_rope(h[:, kcols], cos, sin_a, sin_b), kbd_ref, j)
        spread(h[:, vcols], vbd_ref, j)


def _swa_proj(x, g_pre, w_in, cos, sin_a, sin_b):
    bsz, seq, _ = x.shape
    tm = TOKEN_TILE
    tok = lambda w: pl.BlockSpec((1, tm, w), lambda b, i: (b, i, 0))
    tab = lambda: pl.BlockSpec((tm, LANES), lambda b, i: (i, 0))
    return pl.pallas_call(
        _swa_proj_body,
        grid=(bsz, seq // tm),
        in_specs=[tok(D_MODEL), _const_spec((1, D_MODEL)), _const_spec((D_MODEL, SWA_Q + 2 * SWA_KV)),
                  tab(), tab(), tab()],
        out_specs=[tok(SWA_Q), tok(SWA_BD), tok(SWA_BD)],
        out_shape=[
            jax.ShapeDtypeStruct((bsz, seq, SWA_Q), BF16),
            jax.ShapeDtypeStruct((bsz, seq, SWA_BD), BF16),
            jax.ShapeDtypeStruct((bsz, seq, SWA_BD), BF16),
        ],
        compiler_params=_params(("parallel", "parallel")),
        name="swa_proj",
    )(x, g_pre, w_in, cos, sin_a, sin_b)


def _swa_attn_body(nq, sink_ref, q_ref, kp_ref, km_ref, kn_ref, vp_ref, vm_ref, vn_ref,
                   x_ref, gpost_ref, wout_ref, y_ref, s_ref, p_ref, o_ref, inv_ref):
    qb = SWA_QB
    n = pl.program_id(1)
    blk = SWA_BLOCK
    row = lax.broadcasted_iota(jnp.int32, (blk, 2 * LANES), 0)
    col = lax.broadcasted_iota(jnp.int32, (blk, 2 * LANES), 1) & (LANES - 1)
    tri_prev = col >= row
    tri_next = col <= row
    head_a = lax.broadcasted_iota(jnp.int32, (blk, 2 * LANES), 1) < LANES
    low = lax.broadcasted_iota(jnp.int32, (blk, LANES), 1) < SWA_HD

    def key_block(prev_ref, main_ref, next_ref, j, lanes):
        if j == 0:
            return prev_ref[0, :, lanes]
        if j == qb + 1:
            return next_ref[0, :, lanes]
        return main_ref[0, (j - 1) * blk:j * blk, lanes]

    for hk in range(SWA_KV_HEADS):
        bd = slice(2 * hk * LANES, (2 * hk + 2) * LANES)
        pair_lanes = [slice((2 * hk + e) * LANES, (2 * hk + e + 1) * LANES) for e in range(2)]
        for j in range(qb + 2):
            i0, i1 = max(j - 2, 0), min(j, qb - 1)
            kj = key_block(kp_ref, km_ref, kn_ref, j, bd)
            rhs = jnp.concatenate([kj[:, :LANES], kj[:, LANES:]], axis=0)
            lhs = jnp.concatenate([q_ref[0, i0 * blk:(i1 + 1) * blk, pl_] for pl_ in pair_lanes], axis=0)
            s = lax.dot_general(lhs, rhs, (((1,), (1,)), ((), ())), preferred_element_type=F32)
            nrow = (i1 - i0 + 1) * blk
            for e in range(2):
                for i in range(i0, i1 + 1):
                    r0 = e * nrow + (i - i0) * blk
                    s_ref[e, i, j - i] = s[r0:r0 + blk, :]
        for e in range(2):
            sink_a = sink_ref[4 * hk + 2 * e] * LOG2E
            sink_b = sink_ref[4 * hk + 2 * e + 1] * LOG2E
            for i in range(qb):
                g = n * qb + i
                sp = jnp.where(jnp.logical_and(tri_prev, g > 0), s_ref[e, i, 0], NEG_BIG)
                sc = s_ref[e, i, 1]
                sn = jnp.where(jnp.logical_and(tri_next, g < nq - 1), s_ref[e, i, 2], NEG_BIG)
                mx = jnp.maximum(jnp.maximum(sp, sc), sn)
                m_a = jnp.maximum(jnp.max(mx[:, :LANES], axis=-1, keepdims=True), sink_a)
                m_b = jnp.maximum(jnp.max(mx[:, LANES:], axis=-1, keepdims=True), sink_b)
                m = jnp.where(head_a, m_a, m_b)
                pp, pc, pn = jnp.exp2(sp - m), jnp.exp2(sc - m), jnp.exp2(sn - m)
                tot = pp + pc + pn
                d_a = jnp.sum(tot[:, :LANES], axis=-1, keepdims=True)
                d_b = jnp.sum(tot[:, LANES:], axis=-1, keepdims=True)
                inv_a = 1.0 / (d_a + jnp.exp2(sink_a - m_a))
                inv_b = 1.0 / (d_b + jnp.exp2(sink_b - m_b))
                inv_ref[i * blk:(i + 1) * blk, pair_lanes[e]] = jnp.where(low, inv_a, inv_b)
                for which, pw in enumerate((pp, pc, pn)):
                    slot = 2 - which
                    p_ref[e, i + which, slot * blk:(slot + 1) * blk, :] = pw.astype(BF16)
        for j in range(qb + 2):
            i0, i1 = max(j - 2, 0), min(j, qb - 1)
            vj = key_block(vp_ref, vm_ref, vn_ref, j, bd)
            rhs = jnp.concatenate([vj[:, :LANES], vj[:, LANES:]], axis=0)
            r0, r1 = (i0 - (j - 2)) * blk, (i1 - (j - 2) + 1) * blk
            lhs = jnp.concatenate([p_ref[e, j, r0:r1, :] for e in range(2)], axis=0)
            o = jnp.dot(lhs, rhs, preferred_element_type=F32)
            nrow = r1 - r0
            for e in range(2):
                for i in range(i0, i1 + 1):
                    part = o[e * nrow + (i - i0) * blk:e * nrow + (i - i0 + 1) * blk, :]
                    rows = slice(i * blk, (i + 1) * blk)
                    if j == i:
                        o_ref[rows, pair_lanes[e]] = part
                    else:
                        o_ref[rows, pair_lanes[e]] += part
    o_all = (o_ref[...] * inv_ref[...]).astype(BF16)
    y = jnp.dot(o_all, wout_ref[...], preferred_element_type=F32)
    y_ref[0] = x_ref[0] + _rms(y, gpost_ref[...])


def _swa_attn(q, k, v, x, sinks, g_post, w_out):
    bsz, seq, _ = x.shape
    nq = seq // SWA_BLOCK
    qb = SWA_QB
    tq = qb * SWA_BLOCK
    cur = lambda b, n: (b, n, 0)
    prev = lambda b, n: (b, jnp.maximum(n * qb - 1, 0), 0)
    nxt = lambda b, n: (b, jnp.minimum((n + 1) * qb, nq - 1), 0)
    halo = lambda im: pl.BlockSpec((1, SWA_BLOCK, SWA_BD), im)
    main = lambda w: pl.BlockSpec((1, tq, w), cur)
    return pl.pallas_call(
        functools.partial(_swa_attn_body, nq),
        grid=(bsz, nq // qb),
        in_specs=[
            pl.BlockSpec(memory_space=pltpu.SMEM),
            main(SWA_Q),
            halo(prev), main(SWA_BD), halo(nxt), halo(prev), main(SWA_BD), halo(nxt),
            main(D_MODEL),
            _const_spec((1, D_MODEL)),
            _const_spec((SWA_Q, D_MODEL)),
        ],
        out_specs=main(D_MODEL),
        out_shape=jax.ShapeDtypeStruct((bsz, seq, D_MODEL), F32),
        scratch_shapes=[
            pltpu.VMEM((2, qb, 3, SWA_BLOCK, 2 * LANES), F32),
            pltpu.VMEM((2, qb + 2, 3 * SWA_BLOCK, 2 * LANES), BF16),
            pltpu.VMEM((tq, SWA_Q), F32),
            pltpu.VMEM((tq, SWA_Q), F32),
        ],
        compiler_params=_params(("parallel", "parallel")),
        name="swa_attn",
    )(sinks, q, k, k, k, v, v, v, x, g_post, w_out)


def _rope_tables(seq):
    inv_freq = ROPE_THETA ** (-(jnp.arange(ROPE_HALF, dtype=F32) * 2.0 / ROPE_DIM))
    ang = jnp.arange(seq, dtype=F32)[:, None] * inv_freq[None, :]
    cos8, sin8 = jnp.cos(ang), jnp.sin(ang)
    ones = jnp.ones((seq, SWA_HD - ROPE_DIM), F32)
    zeros8 = jnp.zeros((seq, ROPE_HALF), F32)
    zeros = jnp.zeros((seq, SWA_HD - ROPE_DIM), F32)
    cos = jnp.concatenate([cos8, cos8, ones], axis=1)
    sin_a = jnp.concatenate([-sin8, zeros8, zeros], axis=1)
    sin_b = jnp.concatenate([zeros8, sin8, zeros], axis=1)
    rep = LANES // SWA_HD
    return tuple(jnp.tile(t, (1, rep)) for t in (cos, sin_a, sin_b))


def _swa_layer(x, g_pre, g_post, w):
    seq = x.shape[1]
    cos, sin_a, sin_b = _rope_tables(seq)
    q, k, v = _swa_proj(x, g_pre, w["win"], cos, sin_a, sin_b)
    return _swa_attn(q, k, v, x, w["sinks"], g_post, w["wout"])


def _ffn_layer(x, g2, w1r, w2):
    bsz, seq, _ = x.shape
    return _ffn(x.reshape(bsz * seq, D_MODEL), g2, w1r, w2).reshape(bsz, seq, D_MODEL)


def _prep_weights(ffn_w1, ffn_w2, gla_w_in, gla_w_gate_f, gla_b_gate_f, gla_w_gate_b, gla_b_gate_b,
                  gla_onorm, gla_w_out, swa_w_in, swa_sinks, swa_w_out):
    w1r = ffn_w1.astype(BF16)
    w2 = ffn_w2.astype(BF16)
    gla = []
    for j in range(gla_w_in.shape[0]):
        n_main = 2 * GLA_QK + 2 * GLA_V
        wgd = jnp.zeros((D_MODEL, LANES), F32).at[:, :2 * GLA_GATE_RANK].set(gla_w_in[j][:, n_main:])
        wgate = jnp.zeros((LANES, 2 * GLA_QK), F32)
        wgate = wgate.at[:GLA_GATE_RANK, :GLA_QK].set(gla_w_gate_f[j])
        wgate = wgate.at[GLA_GATE_RANK:2 * GLA_GATE_RANK, GLA_QK:].set(gla_w_gate_b[j])
        gla.append(dict(
            wm=gla_w_in[j][:, :n_main].astype(BF16),
            wgd=wgd.astype(BF16),
            wgate=wgate.astype(BF16),
            bgate=jnp.concatenate([gla_b_gate_f[j], gla_b_gate_b[j]])[None, :],
            onorm=gla_onorm[j][None, :],
            wout=gla_w_out[j].astype(BF16),
        ))
    swa = []
    for j in range(swa_w_in.shape[0]):
        swa.append(dict(win=swa_w_in[j].astype(BF16), sinks=swa_sinks[j], wout=swa_w_out[j].astype(BF16)))
    return w1r, w2, gla, swa


def _trunk(x, norm_g, w1r, w2, gla, swa):
    for i in range(DEPTH):
        g = norm_g[i]
        x = _ffn_layer(x, g[0:2], w1r[i, 0], w2[i, 0])
        if i % 2 == 0:
            x = _gla_layer(x, g[2:3], g[3:4], gla[i // 2])
        else:
            x = _swa_layer(x, g[2:3], g[3:4], swa[i // 2])
        x = _ffn_layer(x, g[4:6], w1r[i, 1], w2[i, 1])
    return x


def kernel(x_prompt, x_sample, norm_g, ffn_w1, ffn_w2, gla_w_in, gla_w_gate_f, gla_b_gate_f, gla_w_gate_b,
           gla_b_gate_b, gla_onorm, gla_w_out, swa_w_in, swa_sinks, swa_w_out):
    w1r, w2, gla, swa = _prep_weights(ffn_w1, ffn_w2, gla_w_in, gla_w_gate_f, gla_b_gate_f, gla_w_gate_b,
                                      gla_b_gate_b, gla_onorm, gla_w_out, swa_w_in, swa_sinks, swa_w_out)
    y_prompt = _trunk(x_prompt, norm_g, w1r, w2, gla, swa)
    y_sample = _trunk(x_sample, norm_g, w1r, w2, gla, swa)
    return (y_prompt, y_sample)
```

```python
import functools

import jax
import jax.numpy as jnp
from jax import lax
from jax.experimental import pallas as pl
from jax.experimental.pallas import tpu as pltpu

F32 = jnp.float32
BF16 = jnp.bfloat16

D_MODEL = 1024
DEPTH = 4
NORM_EPS = 1e-6

D_FF = 2816
FFN_RES = 0.5
FFN_CHUNK = 256
FFN_NCHUNK = D_FF // FFN_CHUNK
FFN_TILE = 1024
FFN_SUB = 512

GLA_HEADS = 4
GLA_DK = 128
GLA_DV = 256
GLA_QK = GLA_HEADS * GLA_DK
GLA_V = GLA_HEADS * GLA_DV
GLA_GATE_RANK = 16
GLA_TAU = 16.0
GLA_C = 128
GLA_MID = GLA_C // 2 - 1

SWA_Q_HEADS = 16
SWA_KV_HEADS = 4
SWA_GROUP = SWA_Q_HEADS // SWA_KV_HEADS
SWA_HD = 64
SWA_WINDOW = 128
SWA_BLOCK = 128
SWA_Q = SWA_Q_HEADS * SWA_HD
SWA_KV = SWA_KV_HEADS * SWA_HD
ROPE_THETA = 500000.0
ROPE_DIM = SWA_HD // 4
ROPE_HALF = ROPE_DIM // 2
NEG_BIG = -1e30
LOG2E = 1.4426950408889634
SWA_QSCALE = SWA_HD ** -0.5 * LOG2E

LANES = 128
SWA_BD = 2 * LANES * SWA_KV_HEADS
SWA_QB = 4
SWA_ROWS = 32
TOKEN_TILE = 512
GLA_PROJ_TILE = 1024
GLA_PROJ_SUB = 256
GLA_PROJ_COLS = 256
GLA_REC_TILE = 512
VMEM_LIMIT = 56 * 1024 * 1024


def _rms(x, g):
    ms = jnp.mean(x * x, axis=-1, keepdims=True)
    return x * lax.rsqrt(ms + NORM_EPS) * g


def _silu(x):
    return x * (1.0 / (1.0 + jnp.exp(-x)))


def _run_interleaved(first, second):
    order = [((i + 0.5) / len(first), 0, i) for i in range(len(first))]
    order += [((j + 0.5) / len(second), 1, j) for j in range(len(second))]
    for _, which, idx in sorted(order):
        (first, second)[which][idx]()


def _const_spec(shape):
    nd = len(shape)
    return pl.BlockSpec(shape, lambda *_: (0,) * nd, pipeline_mode=pl.Buffered(1))


def _params(sem):
    return pltpu.CompilerParams(dimension_semantics=sem, vmem_limit_bytes=VMEM_LIMIT)


def _ffn_body(x_ref, g_ref, w1_ref, w2_ref, o_ref, h_ref):
    nsub = x_ref.shape[0] // FFN_SUB
    subs = [slice(s * FFN_SUB, (s + 1) * FFN_SUB) for s in range(nsub)]
    xns = [_rms(x_ref[rows, :], g_ref[0:1, :]).astype(BF16) for rows in subs]
    ys = []
    for rows, xn in zip(subs, xns):
        for c in range(FFN_NCHUNK):
            cols = slice(c * FFN_CHUNK, (c + 1) * FFN_CHUNK)
            gate = jnp.dot(xn, w1_ref[:, cols], preferred_element_type=F32)
            up = jnp.dot(xn, w1_ref[:, D_FF + c * FFN_CHUNK:D_FF + (c + 1) * FFN_CHUNK],
                         preferred_element_type=F32)
            h_ref[rows, cols] = (_silu(gate) * up).astype(BF16)
        ys.append(jnp.dot(h_ref[rows, :], w2_ref[...], preferred_element_type=F32))
    for rows, y in zip(subs, ys):
        o_ref[rows, :] = x_ref[rows, :] + FFN_RES * _rms(y, g_ref[1:2, :])


def _ffn(x2, g2, w1r, w2):
    t = x2.shape[0]
    tm = FFN_TILE
    return pl.pallas_call(
        _ffn_body,
        grid=(t // tm,),
        in_specs=[
            pl.BlockSpec((tm, D_MODEL), lambda i: (i, 0)),
            _const_spec((2, D_MODEL)),
            _const_spec((D_MODEL, 2 * D_FF)),
            _const_spec((D_FF, D_MODEL)),
        ],
        out_specs=pl.BlockSpec((tm, D_MODEL), lambda i: (i, 0)),
        out_shape=jax.ShapeDtypeStruct((t, D_MODEL), F32),
        scratch_shapes=[pltpu.VMEM((tm, D_FF), BF16)],
        compiler_params=_params(("parallel",)),
        name="ffn",
    )(x2, g2, w1r, w2)


def _gla_proj_body(x_ref, g_ref, wm_ref, wgd_ref, wgate_ref, bgate_ref,
                   qkf_ref, qkb_ref, v_ref, r_ref, dl_ref):
    tm = x_ref.shape[0]
    sub = GLA_PROJ_SUB
    row = lax.broadcasted_iota(jnp.int32, (GLA_C, GLA_C), 0)
    col = lax.broadcasted_iota(jnp.int32, (GLA_C, GLA_C), 1)
    tri_f = (col <= row).astype(BF16)
    tri_b = (col >= row).astype(BF16)

    ncol = (2 * GLA_QK + 2 * GLA_V) // GLA_PROJ_COLS

    def project_pieces(s, out):
        rows = slice(s * sub, (s + 1) * sub)

        def first():
            out["xn"] = _rms(x_ref[rows, :], g_ref[...]).astype(BF16)
            gd = jnp.dot(out["xn"], wgd_ref[...], preferred_element_type=F32).astype(BF16)
            out["z"] = jnp.dot(gd, wgate_ref[...], preferred_element_type=F32) + bgate_ref[...]

        def chunk(j):
            def run():
                cols = slice(j * GLA_PROJ_COLS, (j + 1) * GLA_PROJ_COLS)
                out["h"][j] = jnp.dot(out["xn"], wm_ref[:, cols], preferred_element_type=F32)
            return run

        out["h"] = [None] * ncol
        return [first] + [chunk(j) for j in range(ncol)]

    def finish_pieces(s, src):
        rows = slice(s * sub, (s + 1) * sub)
        hcols = lambda lo, hi: jnp.concatenate(src["h"][lo // GLA_PROJ_COLS:hi // GLA_PROJ_COLS], axis=1)

        def cast_v():
            v_ref[rows, :] = hcols(2 * GLA_QK, 2 * GLA_QK + GLA_V).astype(BF16)

        def cast_r():
            r_ref[rows, :] = hcols(2 * GLA_QK + GLA_V, 2 * GLA_QK + 2 * GLA_V).astype(BF16)

        def unit(cc, d):
            def run():
                tri, out_ref, last, mid = ((tri_f, qkf_ref, GLA_C - 1, GLA_MID),
                                           (tri_b, qkb_ref, 0, GLA_MID + 1))[d]
                c = s * (sub // GLA_C) + cc
                crow = slice(cc * GLA_C, (cc + 1) * GLA_C)
                orow = slice(c * GLA_C, (c + 1) * GLA_C)
                q = hcols(0, GLA_QK)[crow, :] * (GLA_DK ** -0.5)
                k = hcols(GLA_QK, 2 * GLA_QK)[crow, :]
                z = src["z"][crow, d * GLA_QK:(d + 1) * GLA_QK]
                g = (jnp.minimum(z, 0.0) - jnp.log(1.0 + jnp.exp(-jnp.abs(z)))) * (LOG2E / GLA_TAU)
                g_hi = g.astype(BF16)
                g_lo = (g - g_hi.astype(F32)).astype(BF16)
                b = (jnp.dot(tri, g_hi, preferred_element_type=F32)
                     + jnp.dot(tri, g_lo, preferred_element_type=F32))
                b_mid = b[mid:mid + 1, :]
                b_last = b[last:last + 1, :]
                out_ref[orow, 0 * GLA_QK:1 * GLA_QK] = (q * jnp.exp2(b - b_mid)).astype(BF16)
                out_ref[orow, 1 * GLA_QK:2 * GLA_QK] = (q * jnp.exp2(b)).astype(BF16)
                out_ref[orow, 2 * GLA_QK:3 * GLA_QK] = (k * jnp.exp2(b_mid - b)).astype(BF16)
                out_ref[orow, 3 * GLA_QK:4 * GLA_QK] = (k * jnp.exp2(b_last - b)).astype(BF16)
                dl_ref[c, :, d * GLA_QK:(d + 1) * GLA_QK] = jnp.exp2(b_last)
            return run

        units = [unit(cc, d) for cc in range(sub // GLA_C) for d in range(2)]
        return units[:2] + [cast_v] + units[2:] + [cast_r]

    cur = {}
    _run_interleaved(project_pieces(0, cur), [])
    for s in range(tm // sub):
        nxt = {}
        _run_interleaved(project_pieces(s + 1, nxt) if s + 1 < tm // sub else [], finish_pieces(s, cur))
        cur = nxt


def _gla_proj(x2, g_pre, wm, wgd, wgate, bgate):
    t = x2.shape[0]
    tm = GLA_PROJ_TILE
    nc = tm // GLA_C
    tok = lambda w: pl.BlockSpec((tm, w), lambda i: (i, 0))
    return pl.pallas_call(
        _gla_proj_body,
        grid=(t // tm,),
        in_specs=[
            tok(D_MODEL),
            _const_spec((1, D_MODEL)),
            _const_spec((D_MODEL, 2 * GLA_QK + 2 * GLA_V)),
            _const_spec((D_MODEL, LANES)),
            _const_spec((LANES, 2 * GLA_QK)),
            _const_spec((1, 2 * GLA_QK)),
        ],
        out_specs=[
            tok(4 * GLA_QK), tok(4 * GLA_QK), tok(GLA_V), tok(GLA_V),
            pl.BlockSpec((nc, 1, 2 * GLA_QK), lambda i: (i, 0, 0)),
        ],
        out_shape=[
            jax.ShapeDtypeStruct((t, 4 * GLA_QK), BF16),
            jax.ShapeDtypeStruct((t, 4 * GLA_QK), BF16),
            jax.ShapeDtypeStruct((t, GLA_V), BF16),
            jax.ShapeDtypeStruct((t, GLA_V), BF16),
            jax.ShapeDtypeStruct((t // GLA_C, 1, 2 * GLA_QK), F32),
        ],
        compiler_params=_params(("parallel",)),
        name="gla_proj",
    )(x2, g_pre, wm, wgd, wgate, bgate)


def _gla_rec_body(qkf_ref, qkb_ref, vf_ref, vb_ref, dlf_ref, dlb_ref, of_ref, ob_ref, s_ref, a_ref, kv_ref):
    tb = qkf_ref.shape[1]
    nck = tb // GLA_C

    @pl.when(pl.program_id(1) == 0)
    def _():
        s_ref[...] = jnp.zeros_like(s_ref)

    row = lax.broadcasted_iota(jnp.int32, (GLA_C, GLA_C), 0)
    col = lax.broadcasted_iota(jnp.int32, (GLA_C, GLA_C), 1)
    dirs = ((qkf_ref, vf_ref, dlf_ref, of_ref, col <= row), (qkb_ref, vb_ref, dlb_ref, ob_ref, col >= row))
    lanes = lambda part, head: slice(part * GLA_QK + head * GLA_DK, part * GLA_QK + (head + 1) * GLA_DK)
    vlanes = lambda head: slice(head * GLA_DV, (head + 1) * GLA_DV)
    unit = lambda d, c, head: (d * nck + c) * GLA_HEADS + head

    for d, (qk_ref, v_ref, _, _, mask) in enumerate(dirs):
        for c in range(nck):
            rows = slice(c * GLA_C, (c + 1) * GLA_C)
            for head in range(GLA_HEADS):
                scores = lax.dot_general(qk_ref[0, rows, lanes(0, head)], qk_ref[0, rows, lanes(2, head)],
                                         (((1,), (1,)), ((), ())), preferred_element_type=F32)
                a_ref[unit(d, c, head)] = jnp.where(mask, scores, 0.0).astype(BF16)
                kv_ref[unit(d, c, head)] = lax.dot_general(
                    qk_ref[0, rows, lanes(3, head)], v_ref[0, rows, vlanes(head)],
                    (((0,), (0,)), ((), ())), preferred_element_type=F32)

    for step in range(nck):
        for d, (qk_ref, v_ref, dl_ref, o_ref, _) in enumerate(dirs):
            c = step if d == 0 else nck - 1 - step
            rows = slice(c * GLA_C, (c + 1) * GLA_C)
            for head in range(GLA_HEADS):
                state = s_ref[d * GLA_HEADS + head]
                lhs = jnp.concatenate([a_ref[unit(d, c, head)], qk_ref[0, rows, lanes(1, head)]], axis=1)
                rhs = jnp.concatenate([v_ref[0, rows, vlanes(head)], state.astype(BF16)], axis=0)
                o_ref[0, rows, vlanes(head)] = jnp.dot(lhs, rhs, preferred_element_type=F32).astype(BF16)
                dl_row = dl_ref[0, c, :, lanes(d, head)]
                dl_col = jnp.transpose(jnp.broadcast_to(dl_row, (GLA_DK, GLA_DK)))
                s_ref[d * GLA_HEADS + head] = (jnp.concatenate([dl_col, dl_col], axis=1) * state
                                               + kv_ref[unit(d, c, head)])


def _gla_rec(qkf, qkb, v, dl, bsz, seq):
    tb = GLA_REC_TILE
    nb = seq // tb
    nck = tb // GLA_C
    fwd = lambda b, i: (b, i, 0)
    bwd = lambda b, i: (b, nb - 1 - i, 0)
    return pl.pallas_call(
        _gla_rec_body,
        grid=(bsz, nb),
        in_specs=[
            pl.BlockSpec((1, tb, 4 * GLA_QK), fwd),
            pl.BlockSpec((1, tb, 4 * GLA_QK), bwd),
            pl.BlockSpec((1, tb, GLA_V), fwd),
            pl.BlockSpec((1, tb, GLA_V), bwd),
            pl.BlockSpec((1, nck, 1, 2 * GLA_QK), lambda b, i: (b, i, 0, 0)),
            pl.BlockSpec((1, nck, 1, 2 * GLA_QK), lambda b, i: (b, nb - 1 - i, 0, 0)),
        ],
        out_specs=[
            pl.BlockSpec((1, tb, GLA_V), fwd),
            pl.BlockSpec((1, tb, GLA_V), bwd),
        ],
        out_shape=[
            jax.ShapeDtypeStruct((bsz, seq, GLA_V), BF16),
            jax.ShapeDtypeStruct((bsz, seq, GLA_V), BF16),
        ],
        scratch_shapes=[
            pltpu.VMEM((2 * GLA_HEADS, GLA_DK, GLA_DV), F32),
            pltpu.VMEM((2 * nck * GLA_HEADS, GLA_C, GLA_C), BF16),
            pltpu.VMEM((2 * nck * GLA_HEADS, GLA_DK, GLA_DV), F32),
        ],
        compiler_params=_params(("parallel", "arbitrary")),
        name="gla_rec",
    )(qkf, qkb, v, v, dl, dl)


def _gla_out_body(of_ref, ob_ref, r_ref, x_ref, gon_ref, gpost_ref, wout_ref, y_ref):
    o = of_ref[...].astype(F32) + ob_ref[...].astype(F32)
    parts = []
    for head in range(GLA_HEADS):
        oh = o[:, head * GLA_DV:(head + 1) * GLA_DV]
        ms = jnp.mean(oh * oh, axis=-1, keepdims=True)
        parts.append(oh * lax.rsqrt(ms + NORM_EPS))
    on = jnp.concatenate(parts, axis=1) * gon_ref[...]
    gated = (on * _silu(r_ref[...].astype(F32))).astype(BF16)
    y = jnp.dot(gated, wout_ref[...], preferred_element_type=F32)
    y_ref[...] = x_ref[...] + _rms(y, gpost_ref[...])


def _gla_out(o_f, o_b, r, x2, g_onorm, g_post, w_out):
    t = x2.shape[0]
    tm = TOKEN_TILE
    tok = lambda: pl.BlockSpec((tm, D_MODEL), lambda i: (i, 0))
    return pl.pallas_call(
        _gla_out_body,
        grid=(t // tm,),
        in_specs=[tok(), tok(), tok(), tok(),
                  _const_spec((1, GLA_V)), _const_spec((1, D_MODEL)), _const_spec((GLA_V, D_MODEL))],
        out_specs=tok(),
        out_shape=jax.ShapeDtypeStruct((t, D_MODEL), F32),
        compiler_params=_params(("parallel",)),
        name="gla_out",
    )(o_f, o_b, r, x2, g_onorm, g_post, w_out)


def _gla_layer(x, g_pre, g_post, w):
    bsz, seq, _ = x.shape
    x2 = x.reshape(bsz * seq, D_MODEL)
    qkf, qkb, v, r, dl = _gla_proj(x2, g_pre, w["wm"], w["wgd"], w["wgate"], w["bgate"])
    shp = lambda a: a.reshape(bsz, seq, a.shape[-1])
    o_f, o_b = _gla_rec(shp(qkf), shp(qkb), shp(v), dl.reshape(bsz, seq // GLA_C, 1, 2 * GLA_QK), bsz, seq)
    y = _gla_out(o_f.reshape(bsz * seq, GLA_V), o_b.reshape(bsz * seq, GLA_V), r, x2,
                 w["onorm"], g_post, w["wout"])
    return y.reshape(bsz, seq, D_MODEL)


def _rope(z, cos, sin_a, sin_b):
    return (z * cos + pltpu.roll(z, LANES - ROPE_HALF, axis=1) * sin_a
            + pltpu.roll(z, ROPE_HALF, axis=1) * sin_b)


def _swa_proj_body(x_ref, g_ref, w_ref, cos_ref, sa_ref, sb_ref, q_ref, kbd_ref, vbd_ref):
    xn = _rms(x_ref[0], g_ref[...]).astype(BF16)
    h = jnp.dot(xn, w_ref[...], preferred_element_type=F32)
    cos, sin_a, sin_b = cos_ref[...], sa_ref[...], sb_ref[...]
    for j in range(SWA_Q // LANES):
        blk = slice(j * LANES, (j + 1) * LANES)
        q_ref[0, :, blk] = (_rope(h[:, blk], cos, sin_a, sin_b) * SWA_QSCALE).astype(BF16)
    low = lax.broadcasted_iota(jnp.int32, (x_ref.shape[1], LANES), 1) < SWA_HD

    def spread(z, out_ref, j):
        zr = pltpu.roll(z, SWA_HD, axis=1)
        parts = (jnp.where(low, z, 0.0), jnp.where(low, 0.0, zr), jnp.where(low, zr, 0.0), jnp.where(low, 0.0, z))
        for i, part in enumerate(parts):
            out_ref[0, :, (4 * j + i) * LANES:(4 * j + i + 1) * LANES] = part.astype(BF16)

    for j in range(SWA_KV // LANES):
        kcols = slice(SWA_Q + j * LANES, SWA_Q + (j + 1) * LANES)
        vcols = slice(SWA_Q + SWA_KV + j * LANES, SWA_Q + SWA_KV + (j + 1) * LANES)
        spread(_rope(h[:, kcols], cos, sin_a, sin_b), kbd_ref, j)
        spread(h[:, vcols], vbd_ref, j)


def _swa_proj(x, g_pre, w_in, cos, sin_a, sin_b):
    bsz, seq, _ = x.shape
    tm = TOKEN_TILE
    tok = lambda w: pl.BlockSpec((1, tm, w), lambda b, i: (b, i, 0))
    tab = lambda: pl.BlockSpec((tm, LANES), lambda b, i: (i, 0))
    return pl.pallas_call(
        _swa_proj_body,
        grid=(bsz, seq // tm),
        in_specs=[tok(D_MODEL), _const_spec((1, D_MODEL)), _const_spec((D_MODEL, SWA_Q + 2 * SWA_KV)),
                  tab(), tab(), tab()],
        out_specs=[tok(SWA_Q), tok(SWA_BD), tok(SWA_BD)],
        out_shape=[
            jax.ShapeDtypeStruct((bsz, seq, SWA_Q), BF16),
            jax.ShapeDtypeStruct((bsz, seq, SWA_BD), BF16),
            jax.ShapeDtypeStruct((bsz, seq, SWA_BD), BF16),
        ],
        compiler_params=_params(("parallel", "parallel")),
        name="swa_proj",
    )(x, g_pre, w_in, cos, sin_a, sin_b)


def _swa_attn_body(nq, sink_ref, q_ref, kp_ref, km_ref, kn_ref, vp_ref, vm_ref, vn_ref,
                   x_ref, gpost_ref, wout_ref, y_ref, s_ref, p_ref, o_ref, inv_ref):
    qb = SWA_QB
    n = pl.program_id(1)
    blk = SWA_BLOCK
    row = lax.broadcasted_iota(jnp.int32, (SWA_ROWS, 2 * LANES), 0)
    lane = lax.broadcasted_iota(jnp.int32, (SWA_ROWS, 2 * LANES), 1)
    col = lane & (LANES - 1)
    head_a = lane < LANES
    low = lax.broadcasted_iota(jnp.int32, (SWA_ROWS, LANES), 1) < SWA_HD

    def key_block(prev_ref, main_ref, next_ref, j, lanes):
        if j == 0:
            return prev_ref[0, :, lanes]
        if j == qb + 1:
            return next_ref[0, :, lanes]
        return main_ref[0, (j - 1) * blk:j * blk, lanes]

    def head_lanes(hk):
        bd = slice(2 * hk * LANES, (2 * hk + 2) * LANES)
        pair_lanes = [slice((2 * hk + e) * LANES, (2 * hk + e + 1) * LANES) for e in range(2)]
        return bd, pair_lanes

    def score_pieces(hk):
        bd, pair_lanes = head_lanes(hk)
        buf = hk % 2

        def piece(j):
            def run():
                i0, i1 = max(j - 2, 0), min(j, qb - 1)
                kj = key_block(kp_ref, km_ref, kn_ref, j, bd)
                rhs = jnp.concatenate([kj[:, :LANES], kj[:, LANES:]], axis=0)
                lhs = jnp.concatenate([q_ref[0, i0 * blk:(i1 + 1) * blk, pl_] for pl_ in pair_lanes], axis=0)
                s = lax.dot_general(lhs, rhs, (((1,), (1,)), ((), ())), preferred_element_type=F32)
                nrow = (i1 - i0 + 1) * blk
                for e in range(2):
                    for i in range(i0, i1 + 1):
                        r0 = e * nrow + (i - i0) * blk
                        s_ref[buf, e, i, j - i] = s[r0:r0 + blk, :]
            return run

        return [piece(j) for j in range(qb + 2)]

    def softmax_pieces(hk):
        _, pair_lanes = head_lanes(hk)
        buf = hk % 2

        def piece(e, i):
            def run():
                sink_a = sink_ref[4 * hk + 2 * e] * LOG2E
                sink_b = sink_ref[4 * hk + 2 * e + 1] * LOG2E
                g = n * qb + i
                for r0 in range(0, blk, SWA_ROWS):
                    rr = slice(r0, r0 + SWA_ROWS)
                    ok_prev = jnp.logical_and(col >= row + r0, g > 0)
                    ok_next = jnp.logical_and(col <= row + r0, g < nq - 1)
                    sp = jnp.where(ok_prev, s_ref[buf, e, i, 0, rr, :], NEG_BIG)
                    sc = s_ref[buf, e, i, 1, rr, :]
                    sn = jnp.where(ok_next, s_ref[buf, e, i, 2, rr, :], NEG_BIG)
                    mx = jnp.maximum(jnp.maximum(sp, sc), sn)
                    m_a = jnp.maximum(jnp.max(mx[:, :LANES], axis=-1, keepdims=True), sink_a)
                    m_b = jnp.maximum(jnp.max(mx[:, LANES:], axis=-1, keepdims=True), sink_b)
                    m = jnp.where(head_a, m_a, m_b)
                    pp, pc, pn = jnp.exp2(sp - m), jnp.exp2(sc - m), jnp.exp2(sn - m)
                    tot = pp + pc + pn
                    d_a = jnp.sum(tot[:, :LANES], axis=-1, keepdims=True)
                    d_b = jnp.sum(tot[:, LANES:], axis=-1, keepdims=True)
                    inv_a = 1.0 / (d_a + jnp.exp2(sink_a - m_a))
                    inv_b = 1.0 / (d_b + jnp.exp2(sink_b - m_b))
                    inv_ref[i * blk + r0:i * blk + r0 + SWA_ROWS, pair_lanes[e]] = jnp.where(low, inv_a, inv_b)
                    for which, pw in enumerate((pp, pc, pn)):
                        slot = 2 - which
                        p_ref[buf, e, i + which, slot * blk + r0:slot * blk + r0 + SWA_ROWS, :] = pw.astype(BF16)
            return run

        return [piece(e, i) for e in range(2) for i in range(qb)]

    def value_pieces(hk):
        bd, pair_lanes = head_lanes(hk)
        buf = hk % 2

        def piece(j):
            def run():
                i0, i1 = max(j - 2, 0), min(j, qb - 1)
                vj = key_block(vp_ref, vm_ref, vn_ref, j, bd)
                rhs = jnp.concatenate([vj[:, :LANES], vj[:, LANES:]], axis=0)
                r0, r1 = (i0 - (j - 2)) * blk, (i1 - (j - 2) + 1) * blk
                lhs = jnp.concatenate([p_ref[buf, e, j, r0:r1, :] for e in range(2)], axis=0)
                o = jnp.dot(lhs, rhs, preferred_element_type=F32)
                nrow = r1 - r0
                for e in range(2):
                    for i in range(i0, i1 + 1):
                        part = o[e * nrow + (i - i0) * blk:e * nrow + (i - i0 + 1) * blk, :]
                        rows = slice(i * blk, (i + 1) * blk)
                        if j == i:
                            o_ref[rows, pair_lanes[e]] = part
                        else:
                            o_ref[rows, pair_lanes[e]] += part
            return run

        return [piece(j) for j in range(qb + 2)]

    for t in range(SWA_KV_HEADS + 2):
        mm = []
        if t < SWA_KV_HEADS:
            mm += score_pieces(t)
        if 0 <= t - 2 < SWA_KV_HEADS:
            mm += value_pieces(t - 2)
        ew = softmax_pieces(t - 1) if 0 <= t - 1 < SWA_KV_HEADS else []
        _run_interleaved(mm, ew)
    o_all = (o_ref[...] * inv_ref[...]).astype(BF16)
    y = jnp.dot(o_all, wout_ref[...], preferred_element_type=F32)
    y_ref[0] = x_ref[0] + _rms(y, gpost_ref[...])


def _swa_attn(q, k, v, x, sinks, g_post, w_out):
    bsz, seq, _ = x.shape
    nq = seq // SWA_BLOCK
    qb = SWA_QB
    tq = qb * SWA_BLOCK
    cur = lambda b, n: (b, n, 0)
    prev = lambda b, n: (b, jnp.maximum(n * qb - 1, 0), 0)
    nxt = lambda b, n: (b, jnp.minimum((n + 1) * qb, nq - 1), 0)
    halo = lambda im: pl.BlockSpec((1, SWA_BLOCK, SWA_BD), im)
    main = lambda w: pl.BlockSpec((1, tq, w), cur)
    return pl.pallas_call(
        functools.partial(_swa_attn_body, nq),
        grid=(bsz, nq // qb),
        in_specs=[
            pl.BlockSpec(memory_space=pltpu.SMEM),
            main(SWA_Q),
            halo(prev), main(SWA_BD), halo(nxt), halo(prev), main(SWA_BD), halo(nxt),
            main(D_MODEL),
            _const_spec((1, D_MODEL)),
            _const_spec((SWA_Q, D_MODEL)),
        ],
        out_specs=main(D_MODEL),
        out_shape=jax.ShapeDtypeStruct((bsz, seq, D_MODEL), F32),
        scratch_shapes=[
            pltpu.VMEM((2, 2, qb, 3, SWA_BLOCK, 2 * LANES), F32),
            pltpu.VMEM((2, 2, qb + 2, 3 * SWA_BLOCK, 2 * LANES), BF16),
            pltpu.VMEM((tq, SWA_Q), F32),
            pltpu.VMEM((tq, SWA_Q), F32),
        ],
        compiler_params=_params(("parallel", "parallel")),
        name="swa_attn",
    )(sinks, q, k, k, k, v, v, v, x, g_post, w_out)


def _rope_tables(seq):
    inv_freq = ROPE_THETA ** (-(jnp.arange(ROPE_HALF, dtype=F32) * 2.0 / ROPE_DIM))
    ang = jnp.arange(seq, dtype=F32)[:, None] * inv_freq[None, :]
    cos8, sin8 = jnp.cos(ang), jnp.sin(ang)
    ones = jnp.ones((seq, SWA_HD - ROPE_DIM), F32)
    zeros8 = jnp.zeros((seq, ROPE_HALF), F32)
    zeros = jnp.zeros((seq, SWA_HD - ROPE_DIM), F32)
    cos = jnp.concatenate([cos8, cos8, ones], axis=1)
    sin_a = jnp.concatenate([-sin8, zeros8, zeros], axis=1)
    sin_b = jnp.concatenate([zeros8, sin8, zeros], axis=1)
    rep = LANES // SWA_HD
    return tuple(jnp.tile(t, (1, rep)) for t in (cos, sin_a, sin_b))


def _swa_layer(x, g_pre, g_post, w):
    seq = x.shape[1]
    cos, sin_a, sin_b = _rope_tables(seq)
    q, k, v = _swa_proj(x, g_pre, w["win"], cos, sin_a, sin_b)
    return _swa_attn(q, k, v, x, w["sinks"], g_post, w["wout"])


def _ffn_layer(x, g2, w1r, w2):
    bsz, seq, _ = x.shape
    return _ffn(x.reshape(bsz * seq, D_MODEL), g2, w1r, w2).reshape(bsz, seq, D_MODEL)


def _prep_weights(ffn_w1, ffn_w2, gla_w_in, gla_w_gate_f, gla_b_gate_f, gla_w_gate_b, gla_b_gate_b,
                  gla_onorm, gla_w_out, swa_w_in, swa_sinks, swa_w_out):
    w1r = [[ffn_w1[i, j].astype(BF16) for j in range(2)] for i in range(DEPTH)]
    w2 = [[ffn_w2[i, j].astype(BF16) for j in range(2)] for i in range(DEPTH)]
    gla = []
    for j in range(gla_w_in.shape[0]):
        n_main = 2 * GLA_QK + 2 * GLA_V
        wgd = jnp.zeros((D_MODEL, LANES), F32).at[:, :2 * GLA_GATE_RANK].set(gla_w_in[j][:, n_main:])
        wgate = jnp.zeros((LANES, 2 * GLA_QK), F32)
        wgate = wgate.at[:GLA_GATE_RANK, :GLA_QK].set(gla_w_gate_f[j])
        wgate = wgate.at[GLA_GATE_RANK:2 * GLA_GATE_RANK, GLA_QK:].set(gla_w_gate_b[j])
        gla.append(dict(
            wm=gla_w_in[j][:, :n_main].astype(BF16),
            wgd=wgd.astype(BF16),
            wgate=wgate.astype(BF16),
            bgate=jnp.concatenate([gla_b_gate_f[j], gla_b_gate_b[j]])[None, :],
            onorm=gla_onorm[j][None, :],
            wout=gla_w_out[j].astype(BF16),
        ))
    swa = []
    for j in range(swa_w_in.shape[0]):
        swa.append(dict(win=swa_w_in[j].astype(BF16), sinks=swa_sinks[j], wout=swa_w_out[j].astype(BF16)))
    return w1r, w2, gla, swa


def _trunk(x, norm_g, w1r, w2, gla, swa):
    for i in range(DEPTH):
        g = norm_g[i]
        x = _ffn_layer(x, g[0:2], w1r[i][0], w2[i][0])
        if i % 2 == 0:
            x = _gla_layer(x, g[2:3], g[3:4], gla[i // 2])
        else:
            x = _swa_layer(x, g[2:3], g[3:4], swa[i // 2])
        x = _ffn_layer(x, g[4:6], w1r[i][1], w2[i][1])
    return x


def kernel(x_prompt, x_sample, norm_g, ffn_w1, ffn_w2, gla_w_in, gla_w_gate_f, gla_b_gate_f, gla_w_gate_b,
           gla_b_gate_b, gla_onorm, gla_w_out, swa_w_in, swa_sinks, swa_w_out):
    w1r, w2, gla, swa = _prep_weights(ffn_w1, ffn_w2, gla_w_in, gla_w_gate_f, gla_b_gate_f, gla_w_gate_b,
                                      gla_b_gate_b, gla_onorm, gla_w_out, swa_w_in, swa_sinks, swa_w_out)
    y_prompt = _trunk(x_prompt, norm_g, w1r, w2, gla, swa)
    y_sample = _trunk(x_sample, norm_g, w1r, w2, gla, swa)
    return (y_prompt, y_sample)
```

```python
import functools

import jax
import jax.numpy as jnp
from jax import lax
from jax.experimental import pallas as pl
from jax.experimental.pallas import tpu as pltpu

F32 = jnp.float32
BF16 = jnp.bfloat16

D_MODEL = 1024
DEPTH = 4
NORM_EPS = 1e-6

D_FF = 2816
FFN_RES = 0.5
FFN_CHUNK = 256
FFN_NCHUNK = D_FF // FFN_CHUNK
FFN_TILE = 1024
FFN_SUB = 512

GLA_HEADS = 4
GLA_DK = 128
GLA_DV = 256
GLA_QK = GLA_HEADS * GLA_DK
GLA_V = GLA_HEADS * GLA_DV
GLA_GATE_RANK = 16
GLA_TAU = 16.0
GLA_C = 128
GLA_MID = GLA_C // 2 - 1

SWA_Q_HEADS = 16
SWA_KV_HEADS = 4
SWA_GROUP = SWA_Q_HEADS // SWA_KV_HEADS
SWA_HD = 64
SWA_WINDOW = 128
SWA_BLOCK = 128
SWA_Q = SWA_Q_HEADS * SWA_HD
SWA_KV = SWA_KV_HEADS * SWA_HD
ROPE_THETA = 500000.0
ROPE_DIM = SWA_HD // 4
ROPE_HALF = ROPE_DIM // 2
NEG_BIG = -1e30
LOG2E = 1.4426950408889634
SWA_QSCALE = SWA_HD ** -0.5 * LOG2E

LANES = 128
SWA_BD = 2 * LANES * SWA_KV_HEADS
SWA_PROJ_TILE = 1024
SWA_PROJ_SUB = 256
SWA_PROJ_COLS = 256
SWA_QB = 4
SWA_ROWS = 128
TOKEN_TILE = 512
GLA_PROJ_TILE = 1024
GLA_PROJ_SUB = 256
GLA_PROJ_COLS = 256
GLA_OUT_TILE = 1024
GLA_OUT_SUB = 256
GLA_OUT_COLS = 256
GLA_REC_TILE = 512
VMEM_LIMIT = 56 * 1024 * 1024


def _rms(x, g):
    ms = jnp.mean(x * x, axis=-1, keepdims=True)
    return x * lax.rsqrt(ms + NORM_EPS) * g


def _silu(x):
    return x * (1.0 / (1.0 + jnp.exp(-x)))


def _run_interleaved(first, second):
    order = [((i + 0.5) / len(first), 0, i) for i in range(len(first))]
    order += [((j + 0.5) / len(second), 1, j) for j in range(len(second))]
    for _, which, idx in sorted(order):
        (first, second)[which][idx]()


def _const_spec(shape):
    nd = len(shape)
    return pl.BlockSpec(shape, lambda *_: (0,) * nd, pipeline_mode=pl.Buffered(1))


def _params(sem, flags=None):
    return pltpu.CompilerParams(dimension_semantics=sem, vmem_limit_bytes=VMEM_LIMIT, flags=flags)


def _ffn_body(x_ref, g_ref, w1_ref, w2_ref, o_ref, h_ref):
    nsub = x_ref.shape[0] // FFN_SUB
    subs = [slice(s * FFN_SUB, (s + 1) * FFN_SUB) for s in range(nsub)]
    xns = [_rms(x_ref[rows, :], g_ref[0:1, :]).astype(BF16) for rows in subs]
    ys = []
    for rows, xn in zip(subs, xns):
        for c in range(FFN_NCHUNK):
            cols = slice(c * FFN_CHUNK, (c + 1) * FFN_CHUNK)
            gate = jnp.dot(xn, w1_ref[:, cols], preferred_element_type=F32)
            up = jnp.dot(xn, w1_ref[:, D_FF + c * FFN_CHUNK:D_FF + (c + 1) * FFN_CHUNK],
                         preferred_element_type=F32)
            h_ref[rows, cols] = (_silu(gate) * up).astype(BF16)
        ys.append(jnp.dot(h_ref[rows, :], w2_ref[...], preferred_element_type=F32))
    for rows, y in zip(subs, ys):
        o_ref[rows, :] = x_ref[rows, :] + FFN_RES * _rms(y, g_ref[1:2, :])


def _ffn(x2, g2, w1r, w2):
    t = x2.shape[0]
    tm = FFN_TILE
    return pl.pallas_call(
        _ffn_body,
        grid=(t // tm,),
        in_specs=[
            pl.BlockSpec((tm, D_MODEL), lambda i: (i, 0)),
            _const_spec((2, D_MODEL)),
            _const_spec((D_MODEL, 2 * D_FF)),
            _const_spec((D_FF, D_MODEL)),
        ],
        out_specs=pl.BlockSpec((tm, D_MODEL), lambda i: (i, 0)),
        out_shape=jax.ShapeDtypeStruct((t, D_MODEL), F32),
        scratch_shapes=[pltpu.VMEM((tm, D_FF), BF16)],
        compiler_params=_params(("parallel",)),
        name="ffn",
    )(x2, g2, w1r, w2)


def _gla_proj_body(x_ref, g_ref, wm_ref, wgd_ref, wgate_ref, bgate_ref,
                   qkf_ref, qkb_ref, v_ref, r_ref, dl_ref):
    tm = x_ref.shape[0]
    sub = GLA_PROJ_SUB
    row = lax.broadcasted_iota(jnp.int32, (GLA_C, GLA_C), 0)
    col = lax.broadcasted_iota(jnp.int32, (GLA_C, GLA_C), 1)
    tri_f = (col <= row).astype(BF16)
    tri_b = (col >= row).astype(BF16)

    ncol = (2 * GLA_QK + 2 * GLA_V) // GLA_PROJ_COLS

    def project_pieces(s, out):
        rows = slice(s * sub, (s + 1) * sub)

        def first():
            out["xn"] = _rms(x_ref[rows, :], g_ref[...]).astype(BF16)
            gd = jnp.dot(out["xn"], wgd_ref[...], preferred_element_type=F32).astype(BF16)
            out["z"] = jnp.dot(gd, wgate_ref[...], preferred_element_type=F32) + bgate_ref[...]

        def chunk(j):
            def run():
                cols = slice(j * GLA_PROJ_COLS, (j + 1) * GLA_PROJ_COLS)
                out["h"][j] = jnp.dot(out["xn"], wm_ref[:, cols], preferred_element_type=F32)
            return run

        out["h"] = [None] * ncol
        return [first] + [chunk(j) for j in range(ncol)]

    def finish_pieces(s, src):
        rows = slice(s * sub, (s + 1) * sub)
        hcols = lambda lo, hi: jnp.concatenate(src["h"][lo // GLA_PROJ_COLS:hi // GLA_PROJ_COLS], axis=1)

        def cast_v():
            v_ref[rows, :] = hcols(2 * GLA_QK, 2 * GLA_QK + GLA_V).astype(BF16)

        def cast_r():
            r_ref[rows, :] = _silu(hcols(2 * GLA_QK + GLA_V, 2 * GLA_QK + 2 * GLA_V)).astype(BF16)

        def unit(cc, d):
            def run():
                tri, out_ref, last, mid = ((tri_f, qkf_ref, GLA_C - 1, GLA_MID),
                                           (tri_b, qkb_ref, 0, GLA_MID + 1))[d]
                c = s * (sub // GLA_C) + cc
                crow = slice(cc * GLA_C, (cc + 1) * GLA_C)
                orow = slice(c * GLA_C, (c + 1) * GLA_C)
                q = hcols(0, GLA_QK)[crow, :] * (GLA_DK ** -0.5)
                k = hcols(GLA_QK, 2 * GLA_QK)[crow, :]
                z = src["z"][crow, d * GLA_QK:(d + 1) * GLA_QK]
                g = (jnp.minimum(z, 0.0) - jnp.log(1.0 + jnp.exp(-jnp.abs(z)))) * (LOG2E / GLA_TAU)
                g_hi = g.astype(BF16)
                g_lo = (g - g_hi.astype(F32)).astype(BF16)
                b = (jnp.dot(tri, g_hi, preferred_element_type=F32)
                     + jnp.dot(tri, g_lo, preferred_element_type=F32))
                b_mid = b[mid:mid + 1, :]
                b_last = b[last:last + 1, :]
                out_ref[orow, 0 * GLA_QK:1 * GLA_QK] = (q * jnp.exp2(b - b_mid)).astype(BF16)
                out_ref[orow, 1 * GLA_QK:2 * GLA_QK] = (q * jnp.exp2(b)).astype(BF16)
                out_ref[orow, 2 * GLA_QK:3 * GLA_QK] = (k * jnp.exp2(b_mid - b)).astype(BF16)
                out_ref[orow, 3 * GLA_QK:4 * GLA_QK] = (k * jnp.exp2(b_last - b)).astype(BF16)
                dl_ref[c, :, d * GLA_QK:(d + 1) * GLA_QK] = jnp.exp2(b_last)
            return run

        units = [unit(cc, d) for cc in range(sub // GLA_C) for d in range(2)]
        return units[:2] + [cast_v] + units[2:] + [cast_r]

    cur = {}
    _run_interleaved(project_pieces(0, cur), [])
    for s in range(tm // sub):
        nxt = {}
        _run_interleaved(project_pieces(s + 1, nxt) if s + 1 < tm // sub else [], finish_pieces(s, cur))
        cur = nxt


def _gla_proj(x2, g_pre, wm, wgd, wgate, bgate):
    t = x2.shape[0]
    tm = GLA_PROJ_TILE
    nc = tm // GLA_C
    tok = lambda w: pl.BlockSpec((tm, w), lambda i: (i, 0))
    return pl.pallas_call(
        _gla_proj_body,
        grid=(t // tm,),
        in_specs=[
            tok(D_MODEL),
            _const_spec((1, D_MODEL)),
            _const_spec((D_MODEL, 2 * GLA_QK + 2 * GLA_V)),
            _const_spec((D_MODEL, LANES)),
            _const_spec((LANES, 2 * GLA_QK)),
            _const_spec((1, 2 * GLA_QK)),
        ],
        out_specs=[
            tok(4 * GLA_QK), tok(4 * GLA_QK), tok(GLA_V), tok(GLA_V),
            pl.BlockSpec((nc, 1, 2 * GLA_QK), lambda i: (i, 0, 0)),
        ],
        out_shape=[
            jax.ShapeDtypeStruct((t, 4 * GLA_QK), BF16),
            jax.ShapeDtypeStruct((t, 4 * GLA_QK), BF16),
            jax.ShapeDtypeStruct((t, GLA_V), BF16),
            jax.ShapeDtypeStruct((t, GLA_V), BF16),
            jax.ShapeDtypeStruct((t // GLA_C, 1, 2 * GLA_QK), F32),
        ],
        compiler_params=_params(("parallel",)),
        name="gla_proj",
    )(x2, g_pre, wm, wgd, wgate, bgate)


def _gla_rec_body(qkf_ref, qkb_ref, vf_ref, vb_ref, dlf_ref, dlb_ref, of_ref, ob_ref, s_ref, a_ref, kv_ref):
    tb = qkf_ref.shape[1]
    nck = tb // GLA_C

    @pl.when(pl.program_id(1) == 0)
    def _():
        s_ref[...] = jnp.zeros_like(s_ref)

    row = lax.broadcasted_iota(jnp.int32, (GLA_C, GLA_C), 0)
    col = lax.broadcasted_iota(jnp.int32, (GLA_C, GLA_C), 1)
    dirs = ((qkf_ref, vf_ref, dlf_ref, of_ref, col <= row), (qkb_ref, vb_ref, dlb_ref, ob_ref, col >= row))
    lanes = lambda part, head: slice(part * GLA_QK + head * GLA_DK, part * GLA_QK + (head + 1) * GLA_DK)
    vlanes = lambda head: slice(head * GLA_DV, (head + 1) * GLA_DV)
    unit = lambda d, c, head: (d * nck + c) * GLA_HEADS + head

    for d, (qk_ref, v_ref, _, _, mask) in enumerate(dirs):
        for c in range(nck):
            rows = slice(c * GLA_C, (c + 1) * GLA_C)
            for head in range(GLA_HEADS):
                scores = lax.dot_general(qk_ref[0, rows, lanes(0, head)], qk_ref[0, rows, lanes(2, head)],
                                         (((1,), (1,)), ((), ())), preferred_element_type=F32)
                a_ref[unit(d, c, head)] = jnp.where(mask, scores, 0.0).astype(BF16)
                kv_ref[unit(d, c, head)] = lax.dot_general(
                    qk_ref[0, rows, lanes(3, head)], v_ref[0, rows, vlanes(head)],
                    (((0,), (0,)), ((), ())), preferred_element_type=F32)

    for step in range(nck):
        for d, (qk_ref, v_ref, dl_ref, o_ref, _) in enumerate(dirs):
            c = step if d == 0 else nck - 1 - step
            rows = slice(c * GLA_C, (c + 1) * GLA_C)
            for head in range(GLA_HEADS):
                state = s_ref[d * GLA_HEADS + head]
                lhs = jnp.concatenate([a_ref[unit(d, c, head)], qk_ref[0, rows, lanes(1, head)]], axis=1)
                rhs = jnp.concatenate([v_ref[0, rows, vlanes(head)], state.astype(BF16)], axis=0)
                o_ref[0, rows, vlanes(head)] = jnp.dot(lhs, rhs, preferred_element_type=F32).astype(BF16)
                dl_row = dl_ref[0, c, :, lanes(d, head)]
                dl_col = jnp.transpose(jnp.broadcast_to(dl_row, (GLA_DK, GLA_DK)))
                s_ref[d * GLA_HEADS + head] = (jnp.concatenate([dl_col, dl_col], axis=1) * state
                                               + kv_ref[unit(d, c, head)])


def _gla_rec(qkf, qkb, v, dl, bsz, seq):
    tb = GLA_REC_TILE
    nb = seq // tb
    nck = tb // GLA_C
    fwd = lambda b, i: (b, i, 0)
    bwd = lambda b, i: (b, nb - 1 - i, 0)
    return pl.pallas_call(
        _gla_rec_body,
        grid=(bsz, nb),
        in_specs=[
            pl.BlockSpec((1, tb, 4 * GLA_QK), fwd),
            pl.BlockSpec((1, tb, 4 * GLA_QK), bwd),
            pl.BlockSpec((1, tb, GLA_V), fwd),
            pl.BlockSpec((1, tb, GLA_V), bwd),
            pl.BlockSpec((1, nck, 1, 2 * GLA_QK), lambda b, i: (b, i, 0, 0)),
            pl.BlockSpec((1, nck, 1, 2 * GLA_QK), lambda b, i: (b, nb - 1 - i, 0, 0)),
        ],
        out_specs=[
            pl.BlockSpec((1, tb, GLA_V), fwd),
            pl.BlockSpec((1, tb, GLA_V), bwd),
        ],
        out_shape=[
            jax.ShapeDtypeStruct((bsz, seq, GLA_V), BF16),
            jax.ShapeDtypeStruct((bsz, seq, GLA_V), BF16),
        ],
        scratch_shapes=[
            pltpu.VMEM((2 * GLA_HEADS, GLA_DK, GLA_DV), F32),
            pltpu.VMEM((2 * nck * GLA_HEADS, GLA_C, GLA_C), BF16),
            pltpu.VMEM((2 * nck * GLA_HEADS, GLA_DK, GLA_DV), F32),
        ],
        compiler_params=_params(("parallel", "arbitrary")),
        name="gla_rec",
    )(qkf, qkb, v, v, dl, dl)


def _gla_out_body(of_ref, ob_ref, r_ref, x_ref, gon_ref, gpost_ref, wout_ref, y_ref):
    sub = GLA_OUT_SUB
    nsub = x_ref.shape[0] // sub
    ncol = D_MODEL // GLA_OUT_COLS
    gated = [[None] * GLA_HEADS for _ in range(nsub)]
    ys = [[None] * ncol for _ in range(nsub)]

    def gate_pieces(s):
        rows = slice(s * sub, (s + 1) * sub)

        def piece(head):
            def run():
                lanes = slice(head * GLA_DV, (head + 1) * GLA_DV)
                oh = of_ref[rows, lanes].astype(F32) + ob_ref[rows, lanes].astype(F32)
                ms = jnp.mean(oh * oh, axis=-1, keepdims=True)
                on = oh * lax.rsqrt(ms + NORM_EPS) * gon_ref[:, lanes]
                gated[s][head] = (on * r_ref[rows, lanes].astype(F32)).astype(BF16)
            return run

        return [piece(head) for head in range(GLA_HEADS)]

    def matmul_pieces(s):
        def piece(j):
            def run():
                cols = slice(j * GLA_OUT_COLS, (j + 1) * GLA_OUT_COLS)
                ys[s][j] = jnp.dot(jnp.concatenate(gated[s], axis=1), wout_ref[:, cols], preferred_element_type=F32)
            return run

        return [piece(j) for j in range(ncol)]

    def residual_pieces(s):
        def run():
            rows = slice(s * sub, (s + 1) * sub)
            y_ref[rows, :] = x_ref[rows, :] + _rms(jnp.concatenate(ys[s], axis=1), gpost_ref[...])

        return [run]

    for t in range(nsub + 2):
        mm = matmul_pieces(t - 1) if 0 <= t - 1 < nsub else []
        ew = (gate_pieces(t) if t < nsub else []) + (residual_pieces(t - 2) if 0 <= t - 2 < nsub else [])
        _run_interleaved(mm, ew)


def _gla_out(o_f, o_b, r, x2, g_onorm, g_post, w_out):
    t = x2.shape[0]
    tm = GLA_OUT_TILE
    tok = lambda: pl.BlockSpec((tm, D_MODEL), lambda i: (i, 0))
    return pl.pallas_call(
        _gla_out_body,
        grid=(t // tm,),
        in_specs=[tok(), tok(), tok(), tok(),
                  _const_spec((1, GLA_V)), _const_spec((1, D_MODEL)), _const_spec((GLA_V, D_MODEL))],
        out_specs=tok(),
        out_shape=jax.ShapeDtypeStruct((t, D_MODEL), F32),
        compiler_params=_params(("parallel",)),
        name="gla_out",
    )(o_f, o_b, r, x2, g_onorm, g_post, w_out)


def _gla_layer(x, g_pre, g_post, w):
    bsz, seq, _ = x.shape
    x2 = x.reshape(bsz * seq, D_MODEL)
    qkf, qkb, v, r, dl = _gla_proj(x2, g_pre, w["wm"], w["wgd"], w["wgate"], w["bgate"])
    shp = lambda a: a.reshape(bsz, seq, a.shape[-1])
    o_f, o_b = _gla_rec(shp(qkf), shp(qkb), shp(v), dl.reshape(bsz, seq // GLA_C, 1, 2 * GLA_QK), bsz, seq)
    y = _gla_out(o_f.reshape(bsz * seq, GLA_V), o_b.reshape(bsz * seq, GLA_V), r, x2,
                 w["onorm"], g_post, w["wout"])
    return y.reshape(bsz, seq, D_MODEL)


def _rope(z, cos, sin_a, sin_b):
    return (z * cos + pltpu.roll(z, LANES - ROPE_HALF, axis=1) * sin_a
            + pltpu.roll(z, ROPE_HALF, axis=1) * sin_b)


def _swa_proj_body(x_ref, g_ref, w_ref, cos_ref, sa_ref, sb_ref, q_ref, kbd_ref, vbd_ref):
    sub = SWA_PROJ_SUB
    nsub = x_ref.shape[1] // sub
    ncol = (SWA_Q + 2 * SWA_KV) // SWA_PROJ_COLS
    per = SWA_PROJ_COLS // LANES
    low = lax.broadcasted_iota(jnp.int32, (sub, LANES), 1) < SWA_HD

    def project_pieces(s, out):
        rows = slice(s * sub, (s + 1) * sub)

        def first():
            out["xn"] = _rms(x_ref[0, rows, :], g_ref[...]).astype(BF16)

        def chunk(j):
            def run():
                cols = slice(j * SWA_PROJ_COLS, (j + 1) * SWA_PROJ_COLS)
                out["h"][j] = jnp.dot(out["xn"], w_ref[:, cols], preferred_element_type=F32)
            return run

        out["h"] = [None] * ncol
        return [first] + [chunk(j) for j in range(ncol)]

    def finish_pieces(s, src):
        rows = slice(s * sub, (s + 1) * sub)
        hblk = lambda b: src["h"][b // per][:, (b % per) * LANES:(b % per + 1) * LANES]
        rope = lambda z: _rope(z, cos_ref[rows, :], sa_ref[rows, :], sb_ref[rows, :])

        def q_piece(j):
            def run():
                q_ref[0, rows, j * LANES:(j + 1) * LANES] = (rope(hblk(j)) * SWA_QSCALE).astype(BF16)
            return run

        def spread(z, out_ref, j):
            zr = pltpu.roll(z, SWA_HD, axis=1)
            parts = (jnp.where(low, z, 0.0), jnp.where(low, 0.0, zr), jnp.where(low, zr, 0.0),
                     jnp.where(low, 0.0, z))
            for i, part in enumerate(parts):
                out_ref[0, rows, (4 * j + i) * LANES:(4 * j + i + 1) * LANES] = part.astype(BF16)

        def k_piece(j):
            return lambda: spread(rope(hblk(SWA_Q // LANES + j)), kbd_ref, j)

        def v_piece(j):
            return lambda: spread(hblk((SWA_Q + SWA_KV) // LANES + j), vbd_ref, j)

        return ([q_piece(j) for j in range(SWA_Q // LANES)] + [k_piece(j) for j in range(SWA_KV // LANES)]
                + [v_piece(j) for j in range(SWA_KV // LANES)])

    cur = {}
    _run_interleaved(project_pieces(0, cur), [])
    for s in range(nsub):
        nxt = {}
        _run_interleaved(project_pieces(s + 1, nxt) if s + 1 < nsub else [], finish_pieces(s, cur))
        cur = nxt


def _swa_proj(x, g_pre, w_in, cos, sin_a, sin_b):
    bsz, seq, _ = x.shape
    tm = SWA_PROJ_TILE
    tok = lambda w: pl.BlockSpec((1, tm, w), lambda b, i: (b, i, 0))
    tab = lambda: pl.BlockSpec((tm, LANES), lambda b, i: (i, 0))
    return pl.pallas_call(
        _swa_proj_body,
        grid=(bsz, seq // tm),
        in_specs=[tok(D_MODEL), _const_spec((1, D_MODEL)), _const_spec((D_MODEL, SWA_Q + 2 * SWA_KV)),
                  tab(), tab(), tab()],
        out_specs=[tok(SWA_Q), tok(SWA_BD), tok(SWA_BD)],
        out_shape=[
            jax.ShapeDtypeStruct((bsz, seq, SWA_Q), BF16),
            jax.ShapeDtypeStruct((bsz, seq, SWA_BD), BF16),
            jax.ShapeDtypeStruct((bsz, seq, SWA_BD), BF16),
        ],
        compiler_params=_params(("parallel", "parallel")),
        name="swa_proj",
    )(x, g_pre, w_in, cos, sin_a, sin_b)


def _swa_attn_body(nq, sink_ref, q_ref, kp_ref, km_ref, kn_ref, vp_ref, vm_ref, vn_ref,
                   x_ref, gpost_ref, wout_ref, y_ref, s_ref, p_ref, o_ref, inv_ref, bias_ref):
    qb = SWA_QB
    n = pl.program_id(1)
    blk = SWA_BLOCK
    row = lax.broadcasted_iota(jnp.int32, (blk, 2 * LANES), 0)
    col = lax.broadcasted_iota(jnp.int32, (blk, 2 * LANES), 1) & (LANES - 1)
    bias_ref[0] = jnp.where(col >= row, 0.0, NEG_BIG).astype(BF16)
    bias_ref[1] = jnp.where(col <= row, 0.0, NEG_BIG).astype(BF16)
    bias_ref[2] = jnp.full((blk, 2 * LANES), NEG_BIG, BF16)
    low = lax.broadcasted_iota(jnp.int32, (SWA_ROWS, LANES), 1) < SWA_HD
    head_a = lax.broadcasted_iota(jnp.int32, (SWA_ROWS, 2 * LANES), 1) < LANES

    def key_block(prev_ref, main_ref, next_ref, j, lanes):
        if j == 0:
            return prev_ref[0, :, lanes]
        if j == qb + 1:
            return next_ref[0, :, lanes]
        return main_ref[0, (j - 1) * blk:j * blk, lanes]

    def head_lanes(hk):
        bd = slice(2 * hk * LANES, (2 * hk + 2) * LANES)
        pair_lanes = [slice((2 * hk + e) * LANES, (2 * hk + e + 1) * LANES) for e in range(2)]
        return bd, pair_lanes

    def score_pieces(hk):
        bd, pair_lanes = head_lanes(hk)
        buf = hk % 2

        def piece(j):
            def run():
                i0, i1 = max(j - 2, 0), min(j, qb - 1)
                kj = key_block(kp_ref, km_ref, kn_ref, j, bd)
                rhs = jnp.concatenate([kj[:, :LANES], kj[:, LANES:]], axis=0)
                lhs = jnp.concatenate([q_ref[0, i0 * blk:(i1 + 1) * blk, pl_] for pl_ in pair_lanes], axis=0)
                s = lax.dot_general(lhs, rhs, (((1,), (1,)), ((), ())), preferred_element_type=F32)
                nrow = (i1 - i0 + 1) * blk
                for e in range(2):
                    for i in range(i0, i1 + 1):
                        r0 = e * nrow + (i - i0) * blk
                        s_ref[buf, e, i, j - i] = s[r0:r0 + blk, :].astype(BF16)
            return run

        return [piece(j) for j in range(qb + 2)]

    def softmax_pieces(hk):
        _, pair_lanes = head_lanes(hk)
        buf = hk % 2

        def piece(e, i):
            def run():
                sink_a = sink_ref[4 * hk + 2 * e] * LOG2E
                sink_b = sink_ref[4 * hk + 2 * e + 1] * LOG2E
                g = n * qb + i
                tbl_prev = 0 if i > 0 else jnp.where(g > 0, 0, 2)
                tbl_next = 1 if i < qb - 1 else jnp.where(g < nq - 1, 1, 2)
                for r0 in range(0, blk, SWA_ROWS):
                    rr = slice(r0, r0 + SWA_ROWS)
                    sp = s_ref[buf, e, i, 0, rr, :] + bias_ref[tbl_prev, rr, :]
                    sc = s_ref[buf, e, i, 1, rr, :]
                    sn = s_ref[buf, e, i, 2, rr, :] + bias_ref[tbl_next, rr, :]
                    mx = jnp.maximum(jnp.maximum(sp, sc), sn).astype(F32)
                    m_a = jnp.maximum(jnp.max(mx[:, :LANES], axis=-1, keepdims=True), sink_a)
                    m_b = jnp.maximum(jnp.max(mx[:, LANES:], axis=-1, keepdims=True), sink_b)
                    m = jnp.where(head_a, m_a, m_b).astype(BF16)
                    probs = [jnp.exp2(sw - m) for sw in (sp, sc, sn)]
                    tot = (probs[0] + probs[1] + probs[2]).astype(F32)
                    m32 = m.astype(F32)
                    inv_a = 1.0 / (jnp.sum(tot[:, :LANES], axis=-1, keepdims=True)
                                   + jnp.exp2(sink_a - m32[:, :1]))
                    inv_b = 1.0 / (jnp.sum(tot[:, LANES:], axis=-1, keepdims=True)
                                   + jnp.exp2(sink_b - m32[:, LANES:LANES + 1]))
                    for which, pw in enumerate(probs):
                        slot = 2 - which
                        p_ref[buf, e, i + which, slot * blk + r0:slot * blk + r0 + SWA_ROWS, :] = pw
                    inv_ref[i * blk + r0:i * blk + r0 + SWA_ROWS, pair_lanes[e]] = jnp.where(low, inv_a, inv_b)
            return run

        return [piece(e, i) for e in range(2) for i in range(qb)]

    def value_pieces(hk):
        bd, pair_lanes = head_lanes(hk)
        buf = hk % 2

        def piece(j):
            def run():
                i0, i1 = max(j - 2, 0), min(j, qb - 1)
                vj = key_block(vp_ref, vm_ref, vn_ref, j, bd)
                rhs = jnp.concatenate([vj[:, :LANES], vj[:, LANES:]], axis=0)
                r0, r1 = (i0 - (j - 2)) * blk, (i1 - (j - 2) + 1) * blk
                lhs = jnp.concatenate([p_ref[buf, e, j, r0:r1, :] for e in range(2)], axis=0)
                o = jnp.dot(lhs, rhs, preferred_element_type=F32)
                nrow = r1 - r0
                for e in range(2):
                    for i in range(i0, i1 + 1):
                        part = o[e * nrow + (i - i0) * blk:e * nrow + (i - i0 + 1) * blk, :]
                        rows = slice(i * blk, (i + 1) * blk)
                        if j == i:
                            o_ref[rows, pair_lanes[e]] = part
                        else:
                            o_ref[rows, pair_lanes[e]] += part
            return run

        return [piece(j) for j in range(qb + 2)]

    for t in range(SWA_KV_HEADS + 2):
        mm = []
        if t < SWA_KV_HEADS:
            mm += score_pieces(t)
        if 0 <= t - 2 < SWA_KV_HEADS:
            mm += value_pieces(t - 2)
        ew = softmax_pieces(t - 1) if 0 <= t - 1 < SWA_KV_HEADS else []
        _run_interleaved(mm, ew)
    o_all = (o_ref[...] * inv_ref[...]).astype(BF16)
    y = jnp.dot(o_all, wout_ref[...], preferred_element_type=F32)
    y_ref[0] = x_ref[0] + _rms(y, gpost_ref[...])


def _swa_attn(q, k, v, x, sinks, g_post, w_out):
    bsz, seq, _ = x.shape
    nq = seq // SWA_BLOCK
    qb = SWA_QB
    tq = qb * SWA_BLOCK
    cur = lambda b, n: (b, n, 0)
    prev = lambda b, n: (b, jnp.maximum(n * qb - 1, 0), 0)
    nxt = lambda b, n: (b, jnp.minimum((n + 1) * qb, nq - 1), 0)
    halo = lambda im: pl.BlockSpec((1, SWA_BLOCK, SWA_BD), im)
    main = lambda w: pl.BlockSpec((1, tq, w), cur)
    return pl.pallas_call(
        functools.partial(_swa_attn_body, nq),
        grid=(bsz, nq // qb),
        in_specs=[
            pl.BlockSpec(memory_space=pltpu.SMEM),
            main(SWA_Q),
            halo(prev), main(SWA_BD), halo(nxt), halo(prev), main(SWA_BD), halo(nxt),
            main(D_MODEL),
            _const_spec((1, D_MODEL)),
            _const_spec((SWA_Q, D_MODEL)),
        ],
        out_specs=main(D_MODEL),
        out_shape=jax.ShapeDtypeStruct((bsz, seq, D_MODEL), F32),
        scratch_shapes=[
            pltpu.VMEM((2, 2, qb, 3, SWA_BLOCK, 2 * LANES), BF16),
            pltpu.VMEM((2, 2, qb + 2, 3 * SWA_BLOCK, 2 * LANES), BF16),
            pltpu.VMEM((tq, SWA_Q), F32),
            pltpu.VMEM((tq, SWA_Q), F32),
            pltpu.VMEM((3, SWA_BLOCK, 2 * LANES), BF16),
        ],
        compiler_params=_params(("parallel", "parallel")),
        name="swa_attn",
    )(sinks, q, k, k, k, v, v, v, x, g_post, w_out)


def _rope_tables(seq):
    inv_freq = ROPE_THETA ** (-(jnp.arange(ROPE_HALF, dtype=F32) * 2.0 / ROPE_DIM))
    ang = jnp.arange(seq, dtype=F32)[:, None] * inv_freq[None, :]
    cos8, sin8 = jnp.cos(ang), jnp.sin(ang)
    ones = jnp.ones((seq, SWA_HD - ROPE_DIM), F32)
    zeros8 = jnp.zeros((seq, ROPE_HALF), F32)
    zeros = jnp.zeros((seq, SWA_HD - ROPE_DIM), F32)
    cos = jnp.concatenate([cos8, cos8, ones], axis=1)
    sin_a = jnp.concatenate([-sin8, zeros8, zeros], axis=1)
    sin_b = jnp.concatenate([zeros8, sin8, zeros], axis=1)
    rep = LANES // SWA_HD
    return tuple(jnp.tile(t, (1, rep)) for t in (cos, sin_a, sin_b))


def _swa_layer(x, g_pre, g_post, w):
    seq = x.shape[1]
    cos, sin_a, sin_b = _rope_tables(seq)
    q, k, v = _swa_proj(x, g_pre, w["win"], cos, sin_a, sin_b)
    return _swa_attn(q, k, v, x, w["sinks"], g_post, w["wout"])


def _ffn_layer(x, g2, w1r, w2):
    bsz, seq, _ = x.shape
    return _ffn(x.reshape(bsz * seq, D_MODEL), g2, w1r, w2).reshape(bsz, seq, D_MODEL)


def _prep_weights(ffn_w1, ffn_w2, gla_w_in, gla_w_gate_f, gla_b_gate_f, gla_w_gate_b, gla_b_gate_b,
                  gla_onorm, gla_w_out, swa_w_in, swa_sinks, swa_w_out):
    w1r = [[ffn_w1[i, j].astype(BF16) for j in range(2)] for i in range(DEPTH)]
    w2 = [[ffn_w2[i, j].astype(BF16) for j in range(2)] for i in range(DEPTH)]
    gla = []
    for j in range(gla_w_in.shape[0]):
        n_main = 2 * GLA_QK + 2 * GLA_V
        wgd = jnp.zeros((D_MODEL, LANES), F32).at[:, :2 * GLA_GATE_RANK].set(gla_w_in[j][:, n_main:])
        wgate = jnp.zeros((LANES, 2 * GLA_QK), F32)
        wgate = wgate.at[:GLA_GATE_RANK, :GLA_QK].set(gla_w_gate_f[j])
        wgate = wgate.at[GLA_GATE_RANK:2 * GLA_GATE_RANK, GLA_QK:].set(gla_w_gate_b[j])
        gla.append(dict(
            wm=gla_w_in[j][:, :n_main].astype(BF16),
            wgd=wgd.astype(BF16),
            wgate=wgate.astype(BF16),
            bgate=jnp.concatenate([gla_b_gate_f[j], gla_b_gate_b[j]])[None, :],
            onorm=gla_onorm[j][None, :],
            wout=gla_w_out[j].astype(BF16),
        ))
    swa = []
    for j in range(swa_w_in.shape[0]):
        swa.append(dict(win=swa_w_in[j].astype(BF16), sinks=swa_sinks[j], wout=swa_w_out[j].astype(BF16)))
    return w1r, w2, gla, swa


def _trunk(x, norm_g, w1r, w2, gla, swa):
    for i in range(DEPTH):
        g = norm_g[i]
        x = _ffn_layer(x, g[0:2], w1r[i][0], w2[i][0])
        if i % 2 == 0:
            x = _gla_layer(x, g[2:3], g[3:4], gla[i // 2])
        else:
            x = _swa_layer(x, g[2:3], g[3:4], swa[i // 2])
        x = _ffn_layer(x, g[4:6], w1r[i][1], w2[i][1])
    return x


def kernel(x_prompt, x_sample, norm_g, ffn_w1, ffn_w2, gla_w_in, gla_w_gate_f, gla_b_gate_f, gla_w_gate_b,
           gla_b_gate_b, gla_onorm, gla_w_out, swa_w_in, swa_sinks, swa_w_out):
    w1r, w2, gla, swa = _prep_weights(ffn_w1, ffn_w2, gla_w_in, gla_w_gate_f, gla_b_gate_f, gla_w_gate_b,
                                      gla_b_gate_b, gla_onorm, gla_w_out, swa_w_in, swa_sinks, swa_w_out)
    y_prompt = _trunk(x_prompt, norm_g, w1r, w2, gla, swa)
    y_sample = _trunk(x_sample, norm_g, w1r, w2, gla, swa)
    return (y_prompt, y_sample)
```

```python
import functools

import jax
import jax.numpy as jnp
from jax import lax
from jax.experimental import pallas as pl
from jax.experimental.pallas import tpu as pltpu

F32 = jnp.float32
BF16 = jnp.bfloat16

D_MODEL = 1024
DEPTH = 4
NORM_EPS = 1e-6

D_FF = 2816
FFN_RES = 0.5
FFN_CHUNK = 256
FFN_NCHUNK = D_FF // FFN_CHUNK
FFN_TILE = 1024
FFN_SUB = 512
GLA_HEADS = 4
GLA_DK = 128
GLA_DV = 256
GLA_QK = GLA_HEADS * GLA_DK
GLA_V = GLA_HEADS * GLA_DV
GLA_GATE_RANK = 16
GLA_TAU = 16.0
GLA_C = 128
GLA_MID = GLA_C // 2 - 1
GLA_NDEC = 3

SWA_Q_HEADS = 16
SWA_KV_HEADS = 4
SWA_GROUP = SWA_Q_HEADS // SWA_KV_HEADS
SWA_HD = 64
SWA_WINDOW = 128
SWA_BLOCK = 128
SWA_Q = SWA_Q_HEADS * SWA_HD
SWA_KV = SWA_KV_HEADS * SWA_HD
ROPE_THETA = 500000.0
ROPE_DIM = SWA_HD // 4
ROPE_HALF = ROPE_DIM // 2
NEG_BIG = -1e30
LOG2E = 1.4426950408889634
SWA_QSCALE = SWA_HD ** -0.5 * LOG2E

LANES = 128
SWA_BD = 2 * LANES * SWA_KV_HEADS
SWA_PROJ_TILE = 1024
SWA_PROJ_SUB = 256
SWA_PROJ_COLS = 256
SWA_QB = 4
SWA_ROWS = 128
TOKEN_TILE = 512
GLA_PROJ_TILE = 1024
GLA_PROJ_SUB = 256
GLA_PROJ_COLS = 256
GLA_OUT_TILE = 1024
GLA_OUT_SUB = 256
GLA_OUT_COLS = 256
GLA_REC_TILE = 512
VMEM_LIMIT = 56 * 1024 * 1024


def _rms(x, g):
    ms = jnp.mean(x * x, axis=-1, keepdims=True)
    return x * lax.rsqrt(ms + NORM_EPS) * g


def _silu(x):
    return x * (1.0 / (1.0 + jnp.exp(-x)))


def _run_interleaved(first, second):
    order = [((i + 0.5) / len(first), 0, i) for i in range(len(first))]
    order += [((j + 0.5) / len(second), 1, j) for j in range(len(second))]
    for _, which, idx in sorted(order):
        (first, second)[which][idx]()


def _const_spec(shape):
    nd = len(shape)
    return pl.BlockSpec(shape, lambda *_: (0,) * nd, pipeline_mode=pl.Buffered(1))


def _params(sem, flags=None):
    return pltpu.CompilerParams(dimension_semantics=sem, vmem_limit_bytes=VMEM_LIMIT, flags=flags)


def _ffn_body(x_ref, g_ref, w1_ref, w2_ref, o_ref, h_ref):
    nsub = x_ref.shape[0] // FFN_SUB
    subs = [slice(s * FFN_SUB, (s + 1) * FFN_SUB) for s in range(nsub)]
    xns = [_rms(x_ref[rows, :], g_ref[0:1, :]).astype(BF16) for rows in subs]
    ys = []
    for rows, xn in zip(subs, xns):
        for c in range(FFN_NCHUNK):
            cols = slice(c * FFN_CHUNK, (c + 1) * FFN_CHUNK)
            gate = jnp.dot(xn, w1_ref[:, cols], preferred_element_type=F32)
            up = jnp.dot(xn, w1_ref[:, D_FF + c * FFN_CHUNK:D_FF + (c + 1) * FFN_CHUNK],
                         preferred_element_type=F32)
            h_ref[rows, cols] = (_silu(gate) * up).astype(BF16)
        ys.append(jnp.dot(h_ref[rows, :], w2_ref[...], preferred_element_type=F32))
    for rows, y in zip(subs, ys):
        o_ref[rows, :] = x_ref[rows, :] + FFN_RES * _rms(y, g_ref[1:2, :])


def _ffn(x2, g2, w1r, w2, layer, which):
    t = x2.shape[0]
    tm = FFN_TILE
    pick = lambda shape: pl.BlockSpec((None, None) + shape, lambda i: (layer, which, 0, 0),
                                      pipeline_mode=pl.Buffered(1))
    return pl.pallas_call(
        _ffn_body,
        grid=(t // tm,),
        in_specs=[
            pl.BlockSpec((tm, D_MODEL), lambda i: (i, 0)),
            _const_spec((2, D_MODEL)),
            pick((D_MODEL, 2 * D_FF)),
            pick((D_FF, D_MODEL)),
        ],
        out_specs=pl.BlockSpec((tm, D_MODEL), lambda i: (i, 0)),
        out_shape=jax.ShapeDtypeStruct((t, D_MODEL), F32),
        scratch_shapes=[pltpu.VMEM((tm, D_FF), BF16)],
        compiler_params=_params(("parallel",)),
        name="ffn",
    )(x2, g2, w1r, w2)


def _gla_proj_body(x_ref, g_ref, wm_ref, wgd_ref, wgate_ref, bgate_ref,
                   qkf_ref, qkb_ref, v_ref, r_ref, dl_ref):
    tm = x_ref.shape[0]
    sub = GLA_PROJ_SUB
    row = lax.broadcasted_iota(jnp.int32, (GLA_C, GLA_C), 0)
    col = lax.broadcasted_iota(jnp.int32, (GLA_C, GLA_C), 1)
    tri_f = (col <= row).astype(BF16)
    tri_b = (col >= row).astype(BF16)

    ncol = (2 * GLA_QK + 2 * GLA_V) // GLA_PROJ_COLS

    def project_pieces(s, out):
        rows = slice(s * sub, (s + 1) * sub)

        def first():
            out["xn"] = _rms(x_ref[rows, :], g_ref[...]).astype(BF16)
            gd = jnp.dot(out["xn"], wgd_ref[...], preferred_element_type=F32).astype(BF16)
            out["z"] = jnp.dot(gd, wgate_ref[...], preferred_element_type=F32) + bgate_ref[...]

        def chunk(j):
            def run():
                cols = slice(j * GLA_PROJ_COLS, (j + 1) * GLA_PROJ_COLS)
                out["h"][j] = jnp.dot(out["xn"], wm_ref[:, cols], preferred_element_type=F32)
            return run

        out["h"] = [None] * ncol
        return [first] + [chunk(j) for j in range(ncol)]

    def finish_pieces(s, src):
        rows = slice(s * sub, (s + 1) * sub)
        hcols = lambda lo, hi: jnp.concatenate(src["h"][lo // GLA_PROJ_COLS:hi // GLA_PROJ_COLS], axis=1)

        def cast_v():
            v_ref[rows, :] = hcols(2 * GLA_QK, 2 * GLA_QK + GLA_V).astype(BF16)

        def cast_r():
            r_ref[rows, :] = _silu(hcols(2 * GLA_QK + GLA_V, 2 * GLA_QK + 2 * GLA_V)).astype(BF16)

        def unit(cc, d):
            def run():
                tri, out_ref, last, mid = ((tri_f, qkf_ref, GLA_C - 1, GLA_MID),
                                           (tri_b, qkb_ref, 0, GLA_MID + 1))[d]
                c = s * (sub // GLA_C) + cc
                crow = slice(cc * GLA_C, (cc + 1) * GLA_C)
                orow = slice(c * GLA_C, (c + 1) * GLA_C)
                q = hcols(0, GLA_QK)[crow, :] * (GLA_DK ** -0.5)
                k = hcols(GLA_QK, 2 * GLA_QK)[crow, :]
                z = src["z"][crow, d * GLA_QK:(d + 1) * GLA_QK]
                g = (jnp.minimum(z, 0.0) - jnp.log(1.0 + jnp.exp(-jnp.abs(z)))) * (LOG2E / GLA_TAU)
                g_hi = g.astype(BF16)
                g_lo = (g - g_hi.astype(F32)).astype(BF16)
                b = (jnp.dot(tri, g_hi, preferred_element_type=F32)
                     + jnp.dot(tri, g_lo, preferred_element_type=F32))
                b_mid = b[mid:mid + 1, :]
                b_last = b[last:last + 1, :]
                out_ref[orow, 0 * GLA_QK:1 * GLA_QK] = (q * jnp.exp2(b - b_mid)).astype(BF16)
                out_ref[orow, 1 * GLA_QK:2 * GLA_QK] = (k * jnp.exp2(b_mid - b)).astype(BF16)
                dcols = slice(d * GLA_QK, (d + 1) * GLA_QK)
                dl_ref[c, 0:1, dcols] = jnp.exp2(b_last)
                dl_ref[c, 1:2, dcols] = jnp.exp2(b_mid)
                dl_ref[c, 2:3, dcols] = jnp.exp2(b_last - b_mid)
            return run

        units = [unit(cc, d) for cc in range(sub // GLA_C) for d in range(2)]
        return units[:2] + [cast_v] + units[2:] + [cast_r]

    cur = {}
    _run_interleaved(project_pieces(0, cur), [])
    for s in range(tm // sub):
        nxt = {}
        _run_interleaved(project_pieces(s + 1, nxt) if s + 1 < tm // sub else [], finish_pieces(s, cur))
        cur = nxt


def _gla_proj(x2, g_pre, wm, wgd, wgate, bgate):
    t = x2.shape[0]
    tm = GLA_PROJ_TILE
    nc = tm // GLA_C
    tok = lambda w: pl.BlockSpec((tm, w), lambda i: (i, 0))
    return pl.pallas_call(
        _gla_proj_body,
        grid=(t // tm,),
        in_specs=[
            tok(D_MODEL),
            _const_spec((1, D_MODEL)),
            _const_spec((D_MODEL, 2 * GLA_QK + 2 * GLA_V)),
            _const_spec((D_MODEL, LANES)),
            _const_spec((LANES, 2 * GLA_QK)),
            _const_spec((1, 2 * GLA_QK)),
        ],
        out_specs=[
            tok(2 * GLA_QK), tok(2 * GLA_QK), tok(GLA_V), tok(GLA_V),
            pl.BlockSpec((nc, GLA_NDEC, 2 * GLA_QK), lambda i: (i, 0, 0)),
        ],
        out_shape=[
            jax.ShapeDtypeStruct((t, 2 * GLA_QK), BF16),
            jax.ShapeDtypeStruct((t, 2 * GLA_QK), BF16),
            jax.ShapeDtypeStruct((t, GLA_V), BF16),
            jax.ShapeDtypeStruct((t, GLA_V), BF16),
            jax.ShapeDtypeStruct((t // GLA_C, GLA_NDEC, 2 * GLA_QK), F32),
        ],
        compiler_params=_params(("parallel",)),
        name="gla_proj",
    )(x2, g_pre, wm, wgd, wgate, bgate)


def _gla_rec_body(qkf_ref, qkb_ref, vf_ref, vb_ref, dlf_ref, dlb_ref, of_ref, ob_ref, s_ref, a_ref, kv_ref):
    tb = qkf_ref.shape[1]
    nck = tb // GLA_C

    @pl.when(pl.program_id(1) == 0)
    def _():
        s_ref[...] = jnp.zeros_like(s_ref)

    row = lax.broadcasted_iota(jnp.int32, (GLA_C, GLA_C), 0)
    col = lax.broadcasted_iota(jnp.int32, (GLA_C, GLA_C), 1)
    dirs = ((qkf_ref, vf_ref, dlf_ref, of_ref, col <= row), (qkb_ref, vb_ref, dlb_ref, ob_ref, col >= row))
    lanes = lambda part, head: slice(part * GLA_QK + head * GLA_DK, part * GLA_QK + (head + 1) * GLA_DK)
    vlanes = lambda head: slice(head * GLA_DV, (head + 1) * GLA_DV)
    unit = lambda d, c, head: (d * nck + c) * GLA_HEADS + head

    for d, (qk_ref, v_ref, dl_ref, _, mask) in enumerate(dirs):
        for c in range(nck):
            rows = slice(c * GLA_C, (c + 1) * GLA_C)
            for head in range(GLA_HEADS):
                k_mid = qk_ref[0, rows, lanes(1, head)]
                scores = lax.dot_general(qk_ref[0, rows, lanes(0, head)], k_mid,
                                         (((1,), (1,)), ((), ())), preferred_element_type=F32)
                a_ref[unit(d, c, head)] = jnp.where(mask, scores, 0.0).astype(BF16)
                k_end = (k_mid.astype(F32) * dl_ref[0, c, 2:3, lanes(d, head)]).astype(BF16)
                kv_ref[unit(d, c, head)] = lax.dot_general(
                    k_end, v_ref[0, rows, vlanes(head)],
                    (((0,), (0,)), ((), ())), preferred_element_type=F32)

    for step in range(nck):
        for d, (qk_ref, v_ref, dl_ref, o_ref, _) in enumerate(dirs):
            c = step if d == 0 else nck - 1 - step
            rows = slice(c * GLA_C, (c + 1) * GLA_C)
            for head in range(GLA_HEADS):
                state = s_ref[d * GLA_HEADS + head]
                q_dec = (qk_ref[0, rows, lanes(0, head)].astype(F32)
                         * dl_ref[0, c, 1:2, lanes(d, head)]).astype(BF16)
                lhs = jnp.concatenate([a_ref[unit(d, c, head)], q_dec], axis=1)
                rhs = jnp.concatenate([v_ref[0, rows, vlanes(head)], state.astype(BF16)], axis=0)
                o_ref[0, rows, vlanes(head)] = jnp.dot(lhs, rhs, preferred_element_type=F32).astype(BF16)
                dl_row = dl_ref[0, c, 0:1, lanes(d, head)]
                dl_col = jnp.transpose(jnp.broadcast_to(dl_row, (GLA_DK, GLA_DK)))
                s_ref[d * GLA_HEADS + head] = (jnp.concatenate([dl_col, dl_col], axis=1) * state
                                               + kv_ref[unit(d, c, head)])


def _gla_rec(qkf, qkb, v, dl, bsz, seq):
    tb = GLA_REC_TILE
    nb = seq // tb
    nck = tb // GLA_C
    fwd = lambda b, i: (b, i, 0)
    bwd = lambda b, i: (b, nb - 1 - i, 0)
    return pl.pallas_call(
        _gla_rec_body,
        grid=(bsz, nb),
        in_specs=[
            pl.BlockSpec((1, tb, 2 * GLA_QK), fwd),
            pl.BlockSpec((1, tb, 2 * GLA_QK), bwd),
            pl.BlockSpec((1, tb, GLA_V), fwd),
            pl.BlockSpec((1, tb, GLA_V), bwd),
            pl.BlockSpec((1, nck, GLA_NDEC, 2 * GLA_QK), lambda b, i: (b, i, 0, 0)),
            pl.BlockSpec((1, nck, GLA_NDEC, 2 * GLA_QK), lambda b, i: (b, nb - 1 - i, 0, 0)),
        ],
        out_specs=[
            pl.BlockSpec((1, tb, GLA_V), fwd),
            pl.BlockSpec((1, tb, GLA_V), bwd),
        ],
        out_shape=[
            jax.ShapeDtypeStruct((bsz, seq, GLA_V), BF16),
            jax.ShapeDtypeStruct((bsz, seq, GLA_V), BF16),
        ],
        scratch_shapes=[
            pltpu.VMEM((2 * GLA_HEADS, GLA_DK, GLA_DV), F32),
            pltpu.VMEM((2 * nck * GLA_HEADS, GLA_C, GLA_C), BF16),
            pltpu.VMEM((2 * nck * GLA_HEADS, GLA_DK, GLA_DV), F32),
        ],
        compiler_params=_params(("parallel", "arbitrary")),
        name="gla_rec",
    )(qkf, qkb, v, v, dl, dl)


def _gla_out_body(of_ref, ob_ref, r_ref, x_ref, gon_ref, gpost_ref, wout_ref, y_ref):
    sub = GLA_OUT_SUB
    nsub = x_ref.shape[0] // sub
    ncol = D_MODEL // GLA_OUT_COLS
    gated = [[None] * GLA_HEADS for _ in range(nsub)]
    ys = [[None] * ncol for _ in range(nsub)]

    def gate_pieces(s):
        rows = slice(s * sub, (s + 1) * sub)

        def piece(head):
            def run():
                lanes = slice(head * GLA_DV, (head + 1) * GLA_DV)
                oh = of_ref[rows, lanes].astype(F32) + ob_ref[rows, lanes].astype(F32)
                ms = jnp.mean(oh * oh, axis=-1, keepdims=True)
                on = oh * lax.rsqrt(ms + NORM_EPS) * gon_ref[:, lanes]
                gated[s][head] = (on * r_ref[rows, lanes].astype(F32)).astype(BF16)
            return run

        return [piece(head) for head in range(GLA_HEADS)]

    def matmul_pieces(s):
        def piece(j):
            def run():
                cols = slice(j * GLA_OUT_COLS, (j + 1) * GLA_OUT_COLS)
                ys[s][j] = jnp.dot(jnp.concatenate(gated[s], axis=1), wout_ref[:, cols], preferred_element_type=F32)
            return run

        return [piece(j) for j in range(ncol)]

    def residual_pieces(s):
        def run():
            rows = slice(s * sub, (s + 1) * sub)
            y_ref[rows, :] = x_ref[rows, :] + _rms(jnp.concatenate(ys[s], axis=1), gpost_ref[...])

        return [run]

    for t in range(nsub + 2):
        mm = matmul_pieces(t - 1) if 0 <= t - 1 < nsub else []
        ew = (gate_pieces(t) if t < nsub else []) + (residual_pieces(t - 2) if 0 <= t - 2 < nsub else [])
        _run_interleaved(mm, ew)


def _gla_out(o_f, o_b, r, x2, g_onorm, g_post, w_out):
    t = x2.shape[0]
    tm = GLA_OUT_TILE
    tok = lambda: pl.BlockSpec((tm, D_MODEL), lambda i: (i, 0))
    return pl.pallas_call(
        _gla_out_body,
        grid=(t // tm,),
        in_specs=[tok(), tok(), tok(), tok(),
                  _const_spec((1, GLA_V)), _const_spec((1, D_MODEL)), _const_spec((GLA_V, D_MODEL))],
        out_specs=tok(),
        out_shape=jax.ShapeDtypeStruct((t, D_MODEL), F32),
        compiler_params=_params(("parallel",)),
        name="gla_out",
    )(o_f, o_b, r, x2, g_onorm, g_post, w_out)


def _gla_layer(x, g_pre, g_post, w):
    bsz, seq, _ = x.shape
    x2 = x.reshape(bsz * seq, D_MODEL)
    qkf, qkb, v, r, dl = _gla_proj(x2, g_pre, w["wm"], w["wgd"], w["wgate"], w["bgate"])
    shp = lambda a: a.reshape(bsz, seq, a.shape[-1])
    o_f, o_b = _gla_rec(shp(qkf), shp(qkb), shp(v), dl.reshape(bsz, seq // GLA_C, GLA_NDEC, 2 * GLA_QK), bsz, seq)
    y = _gla_out(o_f.reshape(bsz * seq, GLA_V), o_b.reshape(bsz * seq, GLA_V), r, x2,
                 w["onorm"], g_post, w["wout"])
    return y.reshape(bsz, seq, D_MODEL)


def _rope(z, cos, sin_a, sin_b):
    return (z * cos + pltpu.roll(z, LANES - ROPE_HALF, axis=1) * sin_a
            + pltpu.roll(z, ROPE_HALF, axis=1) * sin_b)


def _swa_proj_body(x_ref, g_ref, w_ref, cos_ref, sa_ref, sb_ref, q_ref, kbd_ref, vbd_ref):
    sub = SWA_PROJ_SUB
    nsub = x_ref.shape[1] // sub
    ncol = (SWA_Q + 2 * SWA_KV) // SWA_PROJ_COLS
    per = SWA_PROJ_COLS // LANES
    low = lax.broadcasted_iota(jnp.int32, (sub, LANES), 1) < SWA_HD

    def project_pieces(s, out):
        rows = slice(s * sub, (s + 1) * sub)

        def first():
            out["xn"] = _rms(x_ref[0, rows, :], g_ref[...]).astype(BF16)

        def chunk(j):
            def run():
                cols = slice(j * SWA_PROJ_COLS, (j + 1) * SWA_PROJ_COLS)
                out["h"][j] = jnp.dot(out["xn"], w_ref[:, cols], preferred_element_type=F32)
            return run

        out["h"] = [None] * ncol
        return [first] + [chunk(j) for j in range(ncol)]

    def finish_pieces(s, src):
        rows = slice(s * sub, (s + 1) * sub)
        hblk = lambda b: src["h"][b // per][:, (b % per) * LANES:(b % per + 1) * LANES]
        rope = lambda z: _rope(z, cos_ref[rows, :], sa_ref[rows, :], sb_ref[rows, :])

        def q_piece(j):
            def run():
                q_ref[0, rows, j * LANES:(j + 1) * LANES] = (rope(hblk(j)) * SWA_QSCALE).astype(BF16)
            return run

        def spread(z, out_ref, j):
            zr = pltpu.roll(z, SWA_HD, axis=1)
            parts = (jnp.where(low, z, 0.0), jnp.where(low, 0.0, zr), jnp.where(low, zr, 0.0),
                     jnp.where(low, 0.0, z))
            for i, part in enumerate(parts):
                out_ref[0, rows, (4 * j + i) * LANES:(4 * j + i + 1) * LANES] = part.astype(BF16)

        def k_piece(j):
            return lambda: spread(rope(hblk(SWA_Q // LANES + j)), kbd_ref, j)

        def v_piece(j):
            return lambda: spread(hblk((SWA_Q + SWA_KV) // LANES + j), vbd_ref, j)

        return ([q_piece(j) for j in range(SWA_Q // LANES)] + [k_piece(j) for j in range(SWA_KV // LANES)]
                + [v_piece(j) for j in range(SWA_KV // LANES)])

    cur = {}
    _run_interleaved(project_pieces(0, cur), [])
    for s in range(nsub):
        nxt = {}
        _run_interleaved(project_pieces(s + 1, nxt) if s + 1 < nsub else [], finish_pieces(s, cur))
        cur = nxt


def _swa_proj(x, g_pre, w_in, cos, sin_a, sin_b):
    bsz, seq, _ = x.shape
    tm = SWA_PROJ_TILE
    tok = lambda w: pl.BlockSpec((1, tm, w), lambda b, i: (b, i, 0))
    tab = lambda: pl.BlockSpec((tm, LANES), lambda b, i: (i, 0))
    return pl.pallas_call(
        _swa_proj_body,
        grid=(bsz, seq // tm),
        in_specs=[tok(D_MODEL), _const_spec((1, D_MODEL)), _const_spec((D_MODEL, SWA_Q + 2 * SWA_KV)),
                  tab(), tab(), tab()],
        out_specs=[tok(SWA_Q), tok(SWA_BD), tok(SWA_BD)],
        out_shape=[
            jax.ShapeDtypeStruct((bsz, seq, SWA_Q), BF16),
            jax.ShapeDtypeStruct((bsz, seq, SWA_BD), BF16),
            jax.ShapeDtypeStruct((bsz, seq, SWA_BD), BF16),
        ],
        compiler_params=_params(("parallel", "parallel")),
        name="swa_proj",
    )(x, g_pre, w_in, cos, sin_a, sin_b)


def _swa_attn_body(nq, sink_ref, q_ref, kp_ref, km_ref, kn_ref, vp_ref, vm_ref, vn_ref,
                   x_ref, gpost_ref, wout_ref, y_ref, s_ref, p_ref, o_ref, inv_ref, bias_ref):
    qb = SWA_QB
    n = pl.program_id(1)
    blk = SWA_BLOCK
    row = lax.broadcasted_iota(jnp.int32, (blk, 2 * LANES), 0)
    col = lax.broadcasted_iota(jnp.int32, (blk, 2 * LANES), 1) & (LANES - 1)
    bias_ref[0] = jnp.where(col >= row, 0.0, NEG_BIG).astype(BF16)
    bias_ref[1] = jnp.where(col <= row, 0.0, NEG_BIG).astype(BF16)
    bias_ref[2] = jnp.full((blk, 2 * LANES), NEG_BIG, BF16)
    low = lax.broadcasted_iota(jnp.int32, (SWA_ROWS, LANES), 1) < SWA_HD
    head_a = lax.broadcasted_iota(jnp.int32, (SWA_ROWS, 2 * LANES), 1) < LANES

    def key_block(prev_ref, main_ref, next_ref, j, lanes):
        if j == 0:
            return prev_ref[0, :, lanes]
        if j == qb + 1:
            return next_ref[0, :, lanes]
        return main_ref[0, (j - 1) * blk:j * blk, lanes]

    def head_lanes(hk):
        bd = slice(2 * hk * LANES, (2 * hk + 2) * LANES)
        pair_lanes = [slice((2 * hk + e) * LANES, (2 * hk + e + 1) * LANES) for e in range(2)]
        return bd, pair_lanes

    def score_pieces(hk):
        bd, pair_lanes = head_lanes(hk)
        buf = hk % 2

        def piece(j):
            def run():
                i0, i1 = max(j - 2, 0), min(j, qb - 1)
                kj = key_block(kp_ref, km_ref, kn_ref, j, bd)
                rhs = jnp.concatenate([kj[:, :LANES], kj[:, LANES:]], axis=0)
                lhs = jnp.concatenate([q_ref[0, i0 * blk:(i1 + 1) * blk, pl_] for pl_ in pair_lanes], axis=0)
                s = lax.dot_general(lhs, rhs, (((1,), (1,)), ((), ())), preferred_element_type=F32)
                nrow = (i1 - i0 + 1) * blk
                for e in range(2):
                    for i in range(i0, i1 + 1):
                        r0 = e * nrow + (i - i0) * blk
                        s_ref[buf, e, i, j - i] = s[r0:r0 + blk, :].astype(BF16)
            return run

        return [piece(j) for j in range(qb + 2)]

    def softmax_pieces(hk):
        _, pair_lanes = head_lanes(hk)
        buf = hk % 2

        def piece(e, i):
            def run():
                sink_a = sink_ref[4 * hk + 2 * e] * LOG2E
                sink_b = sink_ref[4 * hk + 2 * e + 1] * LOG2E
                g = n * qb + i
                tbl_prev = 0 if i > 0 else jnp.where(g > 0, 0, 2)
                tbl_next = 1 if i < qb - 1 else jnp.where(g < nq - 1, 1, 2)
                for r0 in range(0, blk, SWA_ROWS):
                    rr = slice(r0, r0 + SWA_ROWS)
                    sp = s_ref[buf, e, i, 0, rr, :] + bias_ref[tbl_prev, rr, :]
                    sc = s_ref[buf, e, i, 1, rr, :]
                    sn = s_ref[buf, e, i, 2, rr, :] + bias_ref[tbl_next, rr, :]
                    mx = jnp.maximum(jnp.maximum(sp, sc), sn).astype(F32)
                    m_a = jnp.maximum(jnp.max(mx[:, :LANES], axis=-1, keepdims=True), sink_a)
                    m_b = jnp.maximum(jnp.max(mx[:, LANES:], axis=-1, keepdims=True), sink_b)
                    m = jnp.where(head_a, m_a, m_b).astype(BF16)
                    probs = [jnp.exp2(sw - m) for sw in (sp, sc, sn)]
                    tot = (probs[0] + probs[1] + probs[2]).astype(F32)
                    m32 = m.astype(F32)
                    inv_a = 1.0 / (jnp.sum(tot[:, :LANES], axis=-1, keepdims=True)
                                   + jnp.exp2(sink_a - m32[:, :1]))
                    inv_b = 1.0 / (jnp.sum(tot[:, LANES:], axis=-1, keepdims=True)
                                   + jnp.exp2(sink_b - m32[:, LANES:LANES + 1]))
                    for which, pw in enumerate(probs):
                        slot = 2 - which
                        p_ref[buf, e, i + which, slot * blk + r0:slot * blk + r0 + SWA_ROWS, :] = pw
                    inv_ref[i * blk + r0:i * blk + r0 + SWA_ROWS, pair_lanes[e]] = jnp.where(low, inv_a, inv_b)
            return run

        return [piece(e, i) for e in range(2) for i in range(qb)]

    def value_pieces(hk):
        bd, pair_lanes = head_lanes(hk)
        buf = hk % 2

        def piece(j):
            def run():
                i0, i1 = max(j - 2, 0), min(j, qb - 1)
                vj = key_block(vp_ref, vm_ref, vn_ref, j, bd)
                rhs = jnp.concatenate([vj[:, :LANES], vj[:, LANES:]], axis=0)
                r0, r1 = (i0 - (j - 2)) * blk, (i1 - (j - 2) + 1) * blk
                lhs = jnp.concatenate([p_ref[buf, e, j, r0:r1, :] for e in range(2)], axis=0)
                o = jnp.dot(lhs, rhs, preferred_element_type=F32)
                nrow = r1 - r0
                for e in range(2):
                    for i in range(i0, i1 + 1):
                        part = o[e * nrow + (i - i0) * blk:e * nrow + (i - i0 + 1) * blk, :]
                        rows = slice(i * blk, (i + 1) * blk)
                        if j == i:
                            o_ref[rows, pair_lanes[e]] = part
                        else:
                            o_ref[rows, pair_lanes[e]] += part
            return run

        return [piece(j) for j in range(qb + 2)]

    for t in range(SWA_KV_HEADS + 2):
        mm = []
        if t < SWA_KV_HEADS:
            mm += score_pieces(t)
        if 0 <= t - 2 < SWA_KV_HEADS:
            mm += value_pieces(t - 2)
        ew = softmax_pieces(t - 1) if 0 <= t - 1 < SWA_KV_HEADS else []
        _run_interleaved(mm, ew)
    o_all = (o_ref[...] * inv_ref[...]).astype(BF16)
    y = jnp.dot(o_all, wout_ref[...], preferred_element_type=F32)
    y_ref[0] = x_ref[0] + _rms(y, gpost_ref[...])


def _swa_attn(q, k, v, x, sinks, g_post, w_out):
    bsz, seq, _ = x.shape
    nq = seq // SWA_BLOCK
    qb = SWA_QB
    tq = qb * SWA_BLOCK
    cur = lambda b, n: (b, n, 0)
    prev = lambda b, n: (b, jnp.maximum(n * qb - 1, 0), 0)
    nxt = lambda b, n: (b, jnp.minimum((n + 1) * qb, nq - 1), 0)
    halo = lambda im: pl.BlockSpec((1, SWA_BLOCK, SWA_BD), im)
    main = lambda w: pl.BlockSpec((1, tq, w), cur)
    return pl.pallas_call(
        functools.partial(_swa_attn_body, nq),
        grid=(bsz, nq // qb),
        in_specs=[
            pl.BlockSpec(memory_space=pltpu.SMEM),
            main(SWA_Q),
            halo(prev), main(SWA_BD), halo(nxt), halo(prev), main(SWA_BD), halo(nxt),
            main(D_MODEL),
            _const_spec((1, D_MODEL)),
            _const_spec((SWA_Q, D_MODEL)),
        ],
        out_specs=main(D_MODEL),
        out_shape=jax.ShapeDtypeStruct((bsz, seq, D_MODEL), F32),
        scratch_shapes=[
            pltpu.VMEM((2, 2, qb, 3, SWA_BLOCK, 2 * LANES), BF16),
            pltpu.VMEM((2, 2, qb + 2, 3 * SWA_BLOCK, 2 * LANES), BF16),
            pltpu.VMEM((tq, SWA_Q), F32),
            pltpu.VMEM((tq, SWA_Q), F32),
            pltpu.VMEM((3, SWA_BLOCK, 2 * LANES), BF16),
        ],
        compiler_params=_params(("parallel", "parallel")),
        name="swa_attn",
    )(sinks, q, k, k, k, v, v, v, x, g_post, w_out)


def _rope_tables(seq):
    inv_freq = ROPE_THETA ** (-(jnp.arange(ROPE_HALF, dtype=F32) * 2.0 / ROPE_DIM))
    ang = jnp.arange(seq, dtype=F32)[:, None] * inv_freq[None, :]
    cos8, sin8 = jnp.cos(ang), jnp.sin(ang)
    ones = jnp.ones((seq, SWA_HD - ROPE_DIM), F32)
    zeros8 = jnp.zeros((seq, ROPE_HALF), F32)
    zeros = jnp.zeros((seq, SWA_HD - ROPE_DIM), F32)
    cos = jnp.concatenate([cos8, cos8, ones], axis=1)
    sin_a = jnp.concatenate([-sin8, zeros8, zeros], axis=1)
    sin_b = jnp.concatenate([zeros8, sin8, zeros], axis=1)
    rep = LANES // SWA_HD
    return tuple(jnp.tile(t, (1, rep)) for t in (cos, sin_a, sin_b))


def _swa_layer(x, g_pre, g_post, w):
    seq = x.shape[1]
    cos, sin_a, sin_b = _rope_tables(seq)
    q, k, v = _swa_proj(x, g_pre, w["win"], cos, sin_a, sin_b)
    return _swa_attn(q, k, v, x, w["sinks"], g_post, w["wout"])


def _ffn_layer(x, g2, w1r, w2, layer, which):
    bsz, seq, _ = x.shape
    return _ffn(x.reshape(bsz * seq, D_MODEL), g2, w1r, w2, layer, which).reshape(bsz, seq, D_MODEL)


def _prep_weights(ffn_w1, ffn_w2, gla_w_in, gla_w_gate_f, gla_b_gate_f, gla_w_gate_b, gla_b_gate_b,
                  gla_onorm, gla_w_out, swa_w_in, swa_sinks, swa_w_out):
    w1r = ffn_w1.astype(BF16)
    w2 = ffn_w2.astype(BF16)
    gla = []
    for j in range(gla_w_in.shape[0]):
        n_main = 2 * GLA_QK + 2 * GLA_V
        wgd = jnp.zeros((D_MODEL, LANES), F32).at[:, :2 * GLA_GATE_RANK].set(gla_w_in[j][:, n_main:])
        wgate = jnp.zeros((LANES, 2 * GLA_QK), F32)
        wgate = wgate.at[:GLA_GATE_RANK, :GLA_QK].set(gla_w_gate_f[j])
        wgate = wgate.at[GLA_GATE_RANK:2 * GLA_GATE_RANK, GLA_QK:].set(gla_w_gate_b[j])
        gla.append(dict(
            wm=gla_w_in[j][:, :n_main].astype(BF16),
            wgd=wgd.astype(BF16),
            wgate=wgate.astype(BF16),
            bgate=jnp.concatenate([gla_b_gate_f[j], gla_b_gate_b[j]])[None, :],
            onorm=gla_onorm[j][None, :],
            wout=gla_w_out[j].astype(BF16),
        ))
    swa = []
    for j in range(swa_w_in.shape[0]):
        swa.append(dict(win=swa_w_in[j].astype(BF16), sinks=swa_sinks[j], wout=swa_w_out[j].astype(BF16)))
    return w1r, w2, gla, swa


def _trunk(x, norm_g, w1r, w2, gla, swa):
    for i in range(DEPTH):
        g = norm_g[i]
        x = _ffn_layer(x, g[0:2], w1r, w2, i, 0)
        if i % 2 == 0:
            x = _gla_layer(x, g[2:3], g[3:4], gla[i // 2])
        else:
            x = _swa_layer(x, g[2:3], g[3:4], swa[i // 2])
        x = _ffn_layer(x, g[4:6], w1r, w2, i, 1)
    return x


def kernel(x_prompt, x_sample, norm_g, ffn_w1, ffn_w2, gla_w_in, gla_w_gate_f, gla_b_gate_f, gla_w_gate_b,
           gla_b_gate_b, gla_onorm, gla_w_out, swa_w_in, swa_sinks, swa_w_out):
    w1r, w2, gla, swa = _prep_weights(ffn_w1, ffn_w2, gla_w_in, gla_w_gate_f, gla_b_gate_f, gla_w_gate_b,
                                      gla_b_gate_b, gla_onorm, gla_w_out, swa_w_in, swa_sinks, swa_w_out)
    y_prompt = _trunk(x_prompt, norm_g, w1r, w2, gla, swa)
    y_sample = _trunk(x_sample, norm_g, w1r, w2, gla, swa)
    return (y_prompt, y_sample)
```

```python
import functools

import jax
import jax.numpy as jnp
from jax import lax
from jax.experimental import pallas as pl
from jax.experimental.pallas import tpu as pltpu

F32 = jnp.float32
BF16 = jnp.bfloat16

D_MODEL = 1024
DEPTH = 4
NORM_EPS = 1e-6

D_FF = 2816
FFN_RES = 0.5
FFN_CHUNK = 256
FFN_NCHUNK = D_FF // FFN_CHUNK
FFN_TILE = 1024
FFN_SUB = 512
GLA_HEADS = 4
GLA_DK = 128
GLA_DV = 256
GLA_QK = GLA_HEADS * GLA_DK
GLA_V = GLA_HEADS * GLA_DV
GLA_GATE_RANK = 16
GLA_TAU = 16.0
GLA_C = 128
GLA_MID = GLA_C // 2 - 1
GLA_NDEC = 3

SWA_Q_HEADS = 16
SWA_KV_HEADS = 4
SWA_GROUP = SWA_Q_HEADS // SWA_KV_HEADS
SWA_HD = 64
SWA_WINDOW = 128
SWA_BLOCK = 128
SWA_Q = SWA_Q_HEADS * SWA_HD
SWA_KV = SWA_KV_HEADS * SWA_HD
ROPE_THETA = 500000.0
ROPE_DIM = SWA_HD // 4
ROPE_HALF = ROPE_DIM // 2
NEG_BIG = -1e30
LOG2E = 1.4426950408889634
SWA_QSCALE = SWA_HD ** -0.5 * LOG2E

LANES = 128
SWA_BD = 2 * LANES * SWA_KV_HEADS
SWA_PROJ_TILE = 1024
SWA_PROJ_SUB = 256
SWA_PROJ_COLS = 256
SWA_QB = 4
SWA_ROWS = 128
TOKEN_TILE = 512
GLA_PROJ_TILE = 1024
GLA_PROJ_SUB = 256
GLA_PROJ_COLS = 256
GLA_REC_TILE = 512
VMEM_LIMIT = 56 * 1024 * 1024


def _rms(x, g):
    ms = jnp.mean(x * x, axis=-1, keepdims=True)
    return x * lax.rsqrt(ms + NORM_EPS) * g


def _silu(x):
    return x * (1.0 / (1.0 + jnp.exp(-x)))


def _run_interleaved(first, second):
    order = [((i + 0.5) / len(first), 0, i) for i in range(len(first))]
    order += [((j + 0.5) / len(second), 1, j) for j in range(len(second))]
    for _, which, idx in sorted(order):
        (first, second)[which][idx]()


def _const_spec(shape):
    nd = len(shape)
    return pl.BlockSpec(shape, lambda *_: (0,) * nd, pipeline_mode=pl.Buffered(1))


def _params(sem, flags=None):
    return pltpu.CompilerParams(dimension_semantics=sem, vmem_limit_bytes=VMEM_LIMIT, flags=flags)


def _gla_tail(of_ref, ob_ref, gate_ref, x_ref, gon_ref, gmix_ref, wmix_ref, rows):
    parts = []
    for head in range(GLA_HEADS):
        lanes = slice(head * GLA_DV, (head + 1) * GLA_DV)
        oh = of_ref[rows, lanes].astype(F32) + ob_ref[rows, lanes].astype(F32)
        ms = jnp.mean(oh * oh, axis=-1, keepdims=True)
        on = oh * lax.rsqrt(ms + NORM_EPS) * gon_ref[:, lanes]
        parts.append((on * gate_ref[rows, lanes].astype(F32)).astype(BF16))
    y = jnp.dot(jnp.concatenate(parts, axis=1), wmix_ref[...], preferred_element_type=F32)
    return x_ref[rows, :] + _rms(y, gmix_ref[...])


def _ffn_body(*refs, after_gla):
    if after_gla:
        of_ref, ob_ref, gate_ref, x_ref, gon_ref, gmix_ref, wmix_ref, g_ref, w1_ref, w2_ref, o_ref, h_ref = refs
    else:
        x_ref, g_ref, w1_ref, w2_ref, o_ref, h_ref = refs
    nsub = x_ref.shape[0] // FFN_SUB
    subs = [slice(s * FFN_SUB, (s + 1) * FFN_SUB) for s in range(nsub)]
    xns = []
    for rows in subs:
        if after_gla:
            x1 = _gla_tail(of_ref, ob_ref, gate_ref, x_ref, gon_ref, gmix_ref, wmix_ref, rows)
            o_ref[rows, :] = x1
        else:
            x1 = x_ref[rows, :]
        xns.append(_rms(x1, g_ref[0:1, :]).astype(BF16))
    ys = []
    for xn in xns:
        for c in range(FFN_NCHUNK):
            cols = slice(c * FFN_CHUNK, (c + 1) * FFN_CHUNK)
            gate = jnp.dot(xn, w1_ref[:, cols], preferred_element_type=F32)
            up = jnp.dot(xn, w1_ref[:, D_FF + c * FFN_CHUNK:D_FF + (c + 1) * FFN_CHUNK],
                         preferred_element_type=F32)
            h_ref[:, cols] = (_silu(gate) * up).astype(BF16)
        ys.append(jnp.dot(h_ref[...], w2_ref[...], preferred_element_type=F32))
    base_ref = o_ref if after_gla else x_ref
    for rows, y in zip(subs, ys):
        o_ref[rows, :] = base_ref[rows, :] + FFN_RES * _rms(y, g_ref[1:2, :])


def _ffn(x2, g2, w1r, w2, layer, which, gla=None):
    t = x2.shape[0]
    tm = FFN_TILE
    pick = lambda shape: pl.BlockSpec((None, None) + shape, lambda i: (layer, which, 0, 0),
                                      pipeline_mode=pl.Buffered(1))
    tok = lambda: pl.BlockSpec((tm, D_MODEL), lambda i: (i, 0))
    ffn_specs = [_const_spec((2, D_MODEL)), pick((D_MODEL, 2 * D_FF)), pick((D_FF, D_MODEL))]
    if gla is None:
        operands, in_specs = (x2, g2, w1r, w2), [tok()] + ffn_specs
    else:
        o_f, o_b, gate, g_onorm, g_post, w_out = gla
        operands = (o_f, o_b, gate, x2, g_onorm, g_post, w_out, g2, w1r, w2)
        in_specs = ([tok(), tok(), tok(), tok(), _const_spec((1, GLA_V)), _const_spec((1, D_MODEL)),
                     _const_spec((GLA_V, D_MODEL))] + ffn_specs)
    return pl.pallas_call(
        functools.partial(_ffn_body, after_gla=gla is not None),
        grid=(t // tm,),
        in_specs=in_specs,
        out_specs=tok(),
        out_shape=jax.ShapeDtypeStruct((t, D_MODEL), F32),
        scratch_shapes=[pltpu.VMEM((FFN_SUB, D_FF), BF16)],
        compiler_params=_params(("parallel",)),
        name="ffn_gla" if gla is not None else "ffn",
    )(*operands)


def _gla_proj_body(x_ref, g_ref, wm_ref, wgd_ref, wgate_ref, bgate_ref,
                   qkf_ref, qkb_ref, v_ref, r_ref, dl_ref):
    tm = x_ref.shape[0]
    sub = GLA_PROJ_SUB
    row = lax.broadcasted_iota(jnp.int32, (GLA_C, GLA_C), 0)
    col = lax.broadcasted_iota(jnp.int32, (GLA_C, GLA_C), 1)
    tri_f = (col <= row).astype(BF16)
    tri_b = (col >= row).astype(BF16)

    ncol = (2 * GLA_QK + 2 * GLA_V) // GLA_PROJ_COLS

    def project_pieces(s, out):
        rows = slice(s * sub, (s + 1) * sub)

        def first():
            out["xn"] = _rms(x_ref[rows, :], g_ref[...]).astype(BF16)
            gd = jnp.dot(out["xn"], wgd_ref[...], preferred_element_type=F32).astype(BF16)
            out["z"] = jnp.dot(gd, wgate_ref[...], preferred_element_type=F32) + bgate_ref[...]

        def chunk(j):
            def run():
                cols = slice(j * GLA_PROJ_COLS, (j + 1) * GLA_PROJ_COLS)
                out["h"][j] = jnp.dot(out["xn"], wm_ref[:, cols], preferred_element_type=F32)
            return run

        out["h"] = [None] * ncol
        return [first] + [chunk(j) for j in range(ncol)]

    def finish_pieces(s, src):
        rows = slice(s * sub, (s + 1) * sub)
        hcols = lambda lo, hi: jnp.concatenate(src["h"][lo // GLA_PROJ_COLS:hi // GLA_PROJ_COLS], axis=1)

        def cast_v():
            v_ref[rows, :] = hcols(2 * GLA_QK, 2 * GLA_QK + GLA_V).astype(BF16)

        def cast_r():
            r_ref[rows, :] = _silu(hcols(2 * GLA_QK + GLA_V, 2 * GLA_QK + 2 * GLA_V)).astype(BF16)

        def unit(cc, d):
            def run():
                tri, out_ref, last, mid = ((tri_f, qkf_ref, GLA_C - 1, GLA_MID),
                                           (tri_b, qkb_ref, 0, GLA_MID + 1))[d]
                c = s * (sub // GLA_C) + cc
                crow = slice(cc * GLA_C, (cc + 1) * GLA_C)
                orow = slice(c * GLA_C, (c + 1) * GLA_C)
                q = hcols(0, GLA_QK)[crow, :] * (GLA_DK ** -0.5)
                k = hcols(GLA_QK, 2 * GLA_QK)[crow, :]
                z = src["z"][crow, d * GLA_QK:(d + 1) * GLA_QK]
                g = (jnp.minimum(z, 0.0) - jnp.log(1.0 + jnp.exp(-jnp.abs(z)))) * (LOG2E / GLA_TAU)
                g_hi = g.astype(BF16)
                g_lo = (g - g_hi.astype(F32)).astype(BF16)
                b = (jnp.dot(tri, g_hi, preferred_element_type=F32)
                     + jnp.dot(tri, g_lo, preferred_element_type=F32))
                b_mid = b[mid:mid + 1, :]
                b_last = b[last:last + 1, :]
                out_ref[orow, 0 * GLA_QK:1 * GLA_QK] = (q * jnp.exp2(b - b_mid)).astype(BF16)
                out_ref[orow, 1 * GLA_QK:2 * GLA_QK] = (k * jnp.exp2(b_mid - b)).astype(BF16)
                dcols = slice(d * GLA_QK, (d + 1) * GLA_QK)
                dl_ref[c, 0:1, dcols] = jnp.exp2(b_last)
                dl_ref[c, 1:2, dcols] = jnp.exp2(b_mid)
                dl_ref[c, 2:3, dcols] = jnp.exp2(b_last - b_mid)
            return run

        units = [unit(cc, d) for cc in range(sub // GLA_C) for d in range(2)]
        return units[:2] + [cast_v] + units[2:] + [cast_r]

    cur = {}
    _run_interleaved(project_pieces(0, cur), [])
    for s in range(tm // sub):
        nxt = {}
        _run_interleaved(project_pieces(s + 1, nxt) if s + 1 < tm // sub else [], finish_pieces(s, cur))
        cur = nxt


def _gla_proj(x2, g_pre, wm, wgd, wgate, bgate):
    t = x2.shape[0]
    tm = GLA_PROJ_TILE
    nc = tm // GLA_C
    tok = lambda w: pl.BlockSpec((tm, w), lambda i: (i, 0))
    return pl.pallas_call(
        _gla_proj_body,
        grid=(t // tm,),
        in_specs=[
            tok(D_MODEL),
            _const_spec((1, D_MODEL)),
            _const_spec((D_MODEL, 2 * GLA_QK + 2 * GLA_V)),
            _const_spec((D_MODEL, LANES)),
            _const_spec((LANES, 2 * GLA_QK)),
            _const_spec((1, 2 * GLA_QK)),
        ],
        out_specs=[
            tok(2 * GLA_QK), tok(2 * GLA_QK), tok(GLA_V), tok(GLA_V),
            pl.BlockSpec((nc, GLA_NDEC, 2 * GLA_QK), lambda i: (i, 0, 0)),
        ],
        out_shape=[
            jax.ShapeDtypeStruct((t, 2 * GLA_QK), BF16),
            jax.ShapeDtypeStruct((t, 2 * GLA_QK), BF16),
            jax.ShapeDtypeStruct((t, GLA_V), BF16),
            jax.ShapeDtypeStruct((t, GLA_V), BF16),
            jax.ShapeDtypeStruct((t // GLA_C, GLA_NDEC, 2 * GLA_QK), F32),
        ],
        compiler_params=_params(("parallel",)),
        name="gla_proj",
    )(x2, g_pre, wm, wgd, wgate, bgate)


def _gla_rec_body(qkf_ref, qkb_ref, vf_ref, vb_ref, dlf_ref, dlb_ref, of_ref, ob_ref, s_ref, a_ref, kv_ref):
    tb = qkf_ref.shape[1]
    nck = tb // GLA_C

    @pl.when(pl.program_id(1) == 0)
    def _():
        s_ref[...] = jnp.zeros_like(s_ref)

    row = lax.broadcasted_iota(jnp.int32, (GLA_C, GLA_C), 0)
    col = lax.broadcasted_iota(jnp.int32, (GLA_C, GLA_C), 1)
    dirs = ((qkf_ref, vf_ref, dlf_ref, of_ref, col <= row), (qkb_ref, vb_ref, dlb_ref, ob_ref, col >= row))
    lanes = lambda part, head: slice(part * GLA_QK + head * GLA_DK, part * GLA_QK + (head + 1) * GLA_DK)
    vlanes = lambda head: slice(head * GLA_DV, (head + 1) * GLA_DV)
    unit = lambda d, c, head: (d * nck + c) * GLA_HEADS + head

    for d, (qk_ref, v_ref, dl_ref, _, mask) in enumerate(dirs):
        for c in range(nck):
            rows = slice(c * GLA_C, (c + 1) * GLA_C)
            for head in range(GLA_HEADS):
                k_mid = qk_ref[0, rows, lanes(1, head)]
                scores = lax.dot_general(qk_ref[0, rows, lanes(0, head)], k_mid,
                                         (((1,), (1,)), ((), ())), preferred_element_type=F32)
                a_ref[unit(d, c, head)] = jnp.where(mask, scores, 0.0).astype(BF16)
                k_end = (k_mid.astype(F32) * dl_ref[0, c, 2:3, lanes(d, head)]).astype(BF16)
                kv_ref[unit(d, c, head)] = lax.dot_general(
                    k_end, v_ref[0, rows, vlanes(head)],
                    (((0,), (0,)), ((), ())), preferred_element_type=F32)

    for step in range(nck):
        for d, (qk_ref, v_ref, dl_ref, o_ref, _) in enumerate(dirs):
            c = step if d == 0 else nck - 1 - step
            rows = slice(c * GLA_C, (c + 1) * GLA_C)
            for head in range(GLA_HEADS):
                state = s_ref[d * GLA_HEADS + head]
                q_dec = (qk_ref[0, rows, lanes(0, head)].astype(F32)
                         * dl_ref[0, c, 1:2, lanes(d, head)]).astype(BF16)
                lhs = jnp.concatenate([a_ref[unit(d, c, head)], q_dec], axis=1)
                rhs = jnp.concatenate([v_ref[0, rows, vlanes(head)], state.astype(BF16)], axis=0)
                o_ref[0, rows, vlanes(head)] = jnp.dot(lhs, rhs, preferred_element_type=F32).astype(BF16)
                dl_row = dl_ref[0, c, 0:1, lanes(d, head)]
                dl_col = jnp.transpose(jnp.broadcast_to(dl_row, (GLA_DK, GLA_DK)))
                s_ref[d * GLA_HEADS + head] = (jnp.concatenate([dl_col, dl_col], axis=1) * state
                                               + kv_ref[unit(d, c, head)])


def _gla_rec(qkf, qkb, v, dl, bsz, seq):
    tb = GLA_REC_TILE
    nb = seq // tb
    nck = tb // GLA_C
    fwd = lambda b, i: (b, i, 0)
    bwd = lambda b, i: (b, nb - 1 - i, 0)
    return pl.pallas_call(
        _gla_rec_body,
        grid=(bsz, nb),
        in_specs=[
            pl.BlockSpec((1, tb, 2 * GLA_QK), fwd),
            pl.BlockSpec((1, tb, 2 * GLA_QK), bwd),
            pl.BlockSpec((1, tb, GLA_V), fwd),
            pl.BlockSpec((1, tb, GLA_V), bwd),
            pl.BlockSpec((1, nck, GLA_NDEC, 2 * GLA_QK), lambda b, i: (b, i, 0, 0)),
            pl.BlockSpec((1, nck, GLA_NDEC, 2 * GLA_QK), lambda b, i: (b, nb - 1 - i, 0, 0)),
        ],
        out_specs=[
            pl.BlockSpec((1, tb, GLA_V), fwd),
            pl.BlockSpec((1, tb, GLA_V), bwd),
        ],
        out_shape=[
            jax.ShapeDtypeStruct((bsz, seq, GLA_V), BF16),
            jax.ShapeDtypeStruct((bsz, seq, GLA_V), BF16),
        ],
        scratch_shapes=[
            pltpu.VMEM((2 * GLA_HEADS, GLA_DK, GLA_DV), F32),
            pltpu.VMEM((2 * nck * GLA_HEADS, GLA_C, GLA_C), BF16),
            pltpu.VMEM((2 * nck * GLA_HEADS, GLA_DK, GLA_DV), F32),
        ],
        compiler_params=_params(("parallel", "arbitrary")),
        name="gla_rec",
    )(qkf, qkb, v, v, dl, dl)


def _gla_mixer(x, g_pre, w):
    bsz, seq, _ = x.shape
    x2 = x.reshape(bsz * seq, D_MODEL)
    qkf, qkb, v, gate, dl = _gla_proj(x2, g_pre, w["wm"], w["wgd"], w["wgate"], w["bgate"])
    shp = lambda a: a.reshape(bsz, seq, a.shape[-1])
    o_f, o_b = _gla_rec(shp(qkf), shp(qkb), shp(v), dl.reshape(bsz, seq // GLA_C, GLA_NDEC, 2 * GLA_QK), bsz, seq)
    return o_f.reshape(bsz * seq, GLA_V), o_b.reshape(bsz * seq, GLA_V), gate


def _rope(z, cos, sin_a, sin_b):
    return (z * cos + pltpu.roll(z, LANES - ROPE_HALF, axis=1) * sin_a
            + pltpu.roll(z, ROPE_HALF, axis=1) * sin_b)


def _swa_proj_body(x_ref, g_ref, w_ref, cos_ref, sa_ref, sb_ref, q_ref, kbd_ref, vbd_ref):
    sub = SWA_PROJ_SUB
    nsub = x_ref.shape[1] // sub
    ncol = (SWA_Q + 2 * SWA_KV) // SWA_PROJ_COLS
    per = SWA_PROJ_COLS // LANES
    low = lax.broadcasted_iota(jnp.int32, (sub, LANES), 1) < SWA_HD

    def project_pieces(s, out):
        rows = slice(s * sub, (s + 1) * sub)

        def first():
            out["xn"] = _rms(x_ref[0, rows, :], g_ref[...]).astype(BF16)

        def chunk(j):
            def run():
                cols = slice(j * SWA_PROJ_COLS, (j + 1) * SWA_PROJ_COLS)
                out["h"][j] = jnp.dot(out["xn"], w_ref[:, cols], preferred_element_type=F32)
            return run

        out["h"] = [None] * ncol
        return [first] + [chunk(j) for j in range(ncol)]

    def finish_pieces(s, src):
        rows = slice(s * sub, (s + 1) * sub)
        hblk = lambda b: src["h"][b // per][:, (b % per) * LANES:(b % per + 1) * LANES]
        rope = lambda z: _rope(z, cos_ref[rows, :], sa_ref[rows, :], sb_ref[rows, :])

        def q_piece(j):
            def run():
                q_ref[0, rows, j * LANES:(j + 1) * LANES] = (rope(hblk(j)) * SWA_QSCALE).astype(BF16)
            return run

        def spread(z, out_ref, j):
            zr = pltpu.roll(z, SWA_HD, axis=1)
            parts = (jnp.where(low, z, 0.0), jnp.where(low, 0.0, zr), jnp.where(low, zr, 0.0),
                     jnp.where(low, 0.0, z))
            for i, part in enumerate(parts):
                out_ref[0, rows, (4 * j + i) * LANES:(4 * j + i + 1) * LANES] = part.astype(BF16)

        def k_piece(j):
            return lambda: spread(rope(hblk(SWA_Q // LANES + j)), kbd_ref, j)

        def v_piece(j):
            return lambda: spread(hblk((SWA_Q + SWA_KV) // LANES + j), vbd_ref, j)

        return ([q_piece(j) for j in range(SWA_Q // LANES)] + [k_piece(j) for j in range(SWA_KV // LANES)]
                + [v_piece(j) for j in range(SWA_KV // LANES)])

    cur = {}
    _run_interleaved(project_pieces(0, cur), [])
    for s in range(nsub):
        nxt = {}
        _run_interleaved(project_pieces(s + 1, nxt) if s + 1 < nsub else [], finish_pieces(s, cur))
        cur = nxt


def _swa_proj(x, g_pre, w_in, cos, sin_a, sin_b):
    bsz, seq, _ = x.shape
    tm = SWA_PROJ_TILE
    tok = lambda w: pl.BlockSpec((1, tm, w), lambda b, i: (b, i, 0))
    tab = lambda: pl.BlockSpec((tm, LANES), lambda b, i: (i, 0))
    return pl.pallas_call(
        _swa_proj_body,
        grid=(bsz, seq // tm),
        in_specs=[tok(D_MODEL), _const_spec((1, D_MODEL)), _const_spec((D_MODEL, SWA_Q + 2 * SWA_KV)),
                  tab(), tab(), tab()],
        out_specs=[tok(SWA_Q), tok(SWA_BD), tok(SWA_BD)],
        out_shape=[
            jax.ShapeDtypeStruct((bsz, seq, SWA_Q), BF16),
            jax.ShapeDtypeStruct((bsz, seq, SWA_BD), BF16),
            jax.ShapeDtypeStruct((bsz, seq, SWA_BD), BF16),
        ],
        compiler_params=_params(("parallel", "parallel")),
        name="swa_proj",
    )(x, g_pre, w_in, cos, sin_a, sin_b)


def _swa_attn_body(nq, sink_ref, q_ref, kp_ref, km_ref, kn_ref, vp_ref, vm_ref, vn_ref,
                   x_ref, gpost_ref, wout_ref, y_ref, s_ref, p_ref, o_ref, inv_ref, bias_ref):
    qb = SWA_QB
    n = pl.program_id(1)
    blk = SWA_BLOCK
    row = lax.broadcasted_iota(jnp.int32, (blk, 2 * LANES), 0)
    col = lax.broadcasted_iota(jnp.int32, (blk, 2 * LANES), 1) & (LANES - 1)
    bias_ref[0] = jnp.where(col >= row, 0.0, NEG_BIG).astype(BF16)
    bias_ref[1] = jnp.where(col <= row, 0.0, NEG_BIG).astype(BF16)
    bias_ref[2] = jnp.full((blk, 2 * LANES), NEG_BIG, BF16)
    low = lax.broadcasted_iota(jnp.int32, (SWA_ROWS, LANES), 1) < SWA_HD
    head_a = lax.broadcasted_iota(jnp.int32, (SWA_ROWS, 2 * LANES), 1) < LANES

    def key_block(prev_ref, main_ref, next_ref, j, lanes):
        if j == 0:
            return prev_ref[0, :, lanes]
        if j == qb + 1:
            return next_ref[0, :, lanes]
        return main_ref[0, (j - 1) * blk:j * blk, lanes]

    def head_lanes(hk):
        bd = slice(2 * hk * LANES, (2 * hk + 2) * LANES)
        pair_lanes = [slice((2 * hk + e) * LANES, (2 * hk + e + 1) * LANES) for e in range(2)]
        return bd, pair_lanes

    def score_pieces(hk):
        bd, pair_lanes = head_lanes(hk)
        buf = hk % 2

        def piece(j):
            def run():
                i0, i1 = max(j - 2, 0), min(j, qb - 1)
                kj = key_block(kp_ref, km_ref, kn_ref, j, bd)
                rhs = jnp.concatenate([kj[:, :LANES], kj[:, LANES:]], axis=0)
                lhs = jnp.concatenate([q_ref[0, i0 * blk:(i1 + 1) * blk, pl_] for pl_ in pair_lanes], axis=0)
                s = lax.dot_general(lhs, rhs, (((1,), (1,)), ((), ())), preferred_element_type=F32)
                nrow = (i1 - i0 + 1) * blk
                for e in range(2):
                    for i in range(i0, i1 + 1):
                        r0 = e * nrow + (i - i0) * blk
                        s_ref[buf, e, i, j - i] = s[r0:r0 + blk, :].astype(BF16)
            return run

        return [piece(j) for j in range(qb + 2)]

    def softmax_pieces(hk):
        _, pair_lanes = head_lanes(hk)
        buf = hk % 2

        def piece(e, i):
            def run():
                sink_a = sink_ref[4 * hk + 2 * e] * LOG2E
                sink_b = sink_ref[4 * hk + 2 * e + 1] * LOG2E
                g = n * qb + i
                tbl_prev = 0 if i > 0 else jnp.where(g > 0, 0, 2)
                tbl_next = 1 if i < qb - 1 else jnp.where(g < nq - 1, 1, 2)
                for r0 in range(0, blk, SWA_ROWS):
                    rr = slice(r0, r0 + SWA_ROWS)
                    sp = s_ref[buf, e, i, 0, rr, :] + bias_ref[tbl_prev, rr, :]
                    sc = s_ref[buf, e, i, 1, rr, :]
                    sn = s_ref[buf, e, i, 2, rr, :] + bias_ref[tbl_next, rr, :]
                    mx = jnp.maximum(jnp.maximum(sp, sc), sn).astype(F32)
                    m_a = jnp.maximum(jnp.max(mx[:, :LANES], axis=-1, keepdims=True), sink_a)
                    m_b = jnp.maximum(jnp.max(mx[:, LANES:], axis=-1, keepdims=True), sink_b)
                    m = jnp.where(head_a, m_a, m_b).astype(BF16)
                    probs = [jnp.exp2(sw - m) for sw in (sp, sc, sn)]
                    tot = (probs[0] + probs[1] + probs[2]).astype(F32)
                    m32 = m.astype(F32)
                    inv_a = 1.0 / (jnp.sum(tot[:, :LANES], axis=-1, keepdims=True)
                                   + jnp.exp2(sink_a - m32[:, :1]))
                    inv_b = 1.0 / (jnp.sum(tot[:, LANES:], axis=-1, keepdims=True)
                                   + jnp.exp2(sink_b - m32[:, LANES:LANES + 1]))
                    for which, pw in enumerate(probs):
                        slot = 2 - which
                        p_ref[buf, e, i + which, slot * blk + r0:slot * blk + r0 + SWA_ROWS, :] = pw
                    inv_ref[i * blk + r0:i * blk + r0 + SWA_ROWS, pair_lanes[e]] = jnp.where(low, inv_a, inv_b)
            return run

        return [piece(e, i) for e in range(2) for i in range(qb)]

    def value_pieces(hk):
        bd, pair_lanes = head_lanes(hk)
        buf = hk % 2

        def piece(j):
            def run():
                i0, i1 = max(j - 2, 0), min(j, qb - 1)
                vj = key_block(vp_ref, vm_ref, vn_ref, j, bd)
                rhs = jnp.concatenate([vj[:, :LANES], vj[:, LANES:]], axis=0)
                r0, r1 = (i0 - (j - 2)) * blk, (i1 - (j - 2) + 1) * blk
                lhs = jnp.concatenate([p_ref[buf, e, j, r0:r1, :] for e in range(2)], axis=0)
                o = jnp.dot(lhs, rhs, preferred_element_type=F32)
                nrow = r1 - r0
                for e in range(2):
                    for i in range(i0, i1 + 1):
                        part = o[e * nrow + (i - i0) * blk:e * nrow + (i - i0 + 1) * blk, :]
                        rows = slice(i * blk, (i + 1) * blk)
                        if j == i:
                            o_ref[rows, pair_lanes[e]] = part
                        else:
                            o_ref[rows, pair_lanes[e]] += part
            return run

        return [piece(j) for j in range(qb + 2)]

    for t in range(SWA_KV_HEADS + 2):
        mm = []
        if t < SWA_KV_HEADS:
            mm += score_pieces(t)
        if 0 <= t - 2 < SWA_KV_HEADS:
            mm += value_pieces(t - 2)
        ew = softmax_pieces(t - 1) if 0 <= t - 1 < SWA_KV_HEADS else []
        _run_interleaved(mm, ew)
    o_all = (o_ref[...] * inv_ref[...]).astype(BF16)
    y = jnp.dot(o_all, wout_ref[...], preferred_element_type=F32)
    y_ref[0] = x_ref[0] + _rms(y, gpost_ref[...])


def _swa_attn(q, k, v, x, sinks, g_post, w_out):
    bsz, seq, _ = x.shape
    nq = seq // SWA_BLOCK
    qb = SWA_QB
    tq = qb * SWA_BLOCK
    cur = lambda b, n: (b, n, 0)
    prev = lambda b, n: (b, jnp.maximum(n * qb - 1, 0), 0)
    nxt = lambda b, n: (b, jnp.minimum((n + 1) * qb, nq - 1), 0)
    halo = lambda im: pl.BlockSpec((1, SWA_BLOCK, SWA_BD), im)
    main = lambda w: pl.BlockSpec((1, tq, w), cur)
    return pl.pallas_call(
        functools.partial(_swa_attn_body, nq),
        grid=(bsz, nq // qb),
        in_specs=[
            pl.BlockSpec(memory_space=pltpu.SMEM),
            main(SWA_Q),
            halo(prev), main(SWA_BD), halo(nxt), halo(prev), main(SWA_BD), halo(nxt),
            main(D_MODEL),
            _const_spec((1, D_MODEL)),
            _const_spec((SWA_Q, D_MODEL)),
        ],
        out_specs=main(D_MODEL),
        out_shape=jax.ShapeDtypeStruct((bsz, seq, D_MODEL), F32),
        scratch_shapes=[
            pltpu.VMEM((2, 2, qb, 3, SWA_BLOCK, 2 * LANES), BF16),
            pltpu.VMEM((2, 2, qb + 2, 3 * SWA_BLOCK, 2 * LANES), BF16),
            pltpu.VMEM((tq, SWA_Q), F32),
            pltpu.VMEM((tq, SWA_Q), F32),
            pltpu.VMEM((3, SWA_BLOCK, 2 * LANES), BF16),
        ],
        compiler_params=_params(("parallel", "parallel")),
        name="swa_attn",
    )(sinks, q, k, k, k, v, v, v, x, g_post, w_out)


def _rope_tables(seq):
    inv_freq = ROPE_THETA ** (-(jnp.arange(ROPE_HALF, dtype=F32) * 2.0 / ROPE_DIM))
    ang = jnp.arange(seq, dtype=F32)[:, None] * inv_freq[None, :]
    cos8, sin8 = jnp.cos(ang), jnp.sin(ang)
    ones = jnp.ones((seq, SWA_HD - ROPE_DIM), F32)
    zeros8 = jnp.zeros((seq, ROPE_HALF), F32)
    zeros = jnp.zeros((seq, SWA_HD - ROPE_DIM), F32)
    cos = jnp.concatenate([cos8, cos8, ones], axis=1)
    sin_a = jnp.concatenate([-sin8, zeros8, zeros], axis=1)
    sin_b = jnp.concatenate([zeros8, sin8, zeros], axis=1)
    rep = LANES // SWA_HD
    return tuple(jnp.tile(t, (1, rep)) for t in (cos, sin_a, sin_b))


def _swa_layer(x, g_pre, g_post, w):
    seq = x.shape[1]
    cos, sin_a, sin_b = _rope_tables(seq)
    q, k, v = _swa_proj(x, g_pre, w["win"], cos, sin_a, sin_b)
    return _swa_attn(q, k, v, x, w["sinks"], g_post, w["wout"])


def _ffn_layer(x, g2, w1r, w2, layer, which, gla=None):
    bsz, seq, _ = x.shape
    return _ffn(x.reshape(bsz * seq, D_MODEL), g2, w1r, w2, layer, which, gla).reshape(bsz, seq, D_MODEL)


def _prep_weights(ffn_w1, ffn_w2, gla_w_in, gla_w_gate_f, gla_b_gate_f, gla_w_gate_b, gla_b_gate_b,
                  gla_onorm, gla_w_out, swa_w_in, swa_sinks, swa_w_out):
    w1r = ffn_w1.astype(BF16)
    w2 = ffn_w2.astype(BF16)
    gla = []
    for j in range(gla_w_in.shape[0]):
        n_main = 2 * GLA_QK + 2 * GLA_V
        wgd = jnp.zeros((D_MODEL, LANES), F32).at[:, :2 * GLA_GATE_RANK].set(gla_w_in[j][:, n_main:])
        wgate = jnp.zeros((LANES, 2 * GLA_QK), F32)
        wgate = wgate.at[:GLA_GATE_RANK, :GLA_QK].set(gla_w_gate_f[j])
        wgate = wgate.at[GLA_GATE_RANK:2 * GLA_GATE_RANK, GLA_QK:].set(gla_w_gate_b[j])
        gla.append(dict(
            wm=gla_w_in[j][:, :n_main].astype(BF16),
            wgd=wgd.astype(BF16),
            wgate=wgate.astype(BF16),
            bgate=jnp.concatenate([gla_b_gate_f[j], gla_b_gate_b[j]])[None, :],
            onorm=gla_onorm[j][None, :],
            wout=gla_w_out[j].astype(BF16),
        ))
    swa = []
    for j in range(swa_w_in.shape[0]):
        swa.append(dict(win=swa_w_in[j].astype(BF16), sinks=swa_sinks[j], wout=swa_w_out[j].astype(BF16)))
    return w1r, w2, gla, swa


def _trunk(x, norm_g, w1r, w2, gla, swa):
    for i in range(DEPTH):
        g = norm_g[i]
        x = _ffn_layer(x, g[0:2], w1r, w2, i, 0)
        if i % 2 == 0:
            w = gla[i // 2]
            o_f, o_b, gate = _gla_mixer(x, g[2:3], w)
            x = _ffn_layer(x, g[4:6], w1r, w2, i, 1, gla=(o_f, o_b, gate, w["onorm"], g[3:4], w["wout"]))
        else:
            x = _swa_layer(x, g[2:3], g[3:4], swa[i // 2])
            x = _ffn_layer(x, g[4:6], w1r, w2, i, 1)
    return x


def kernel(x_prompt, x_sample, norm_g, ffn_w1, ffn_w2, gla_w_in, gla_w_gate_f, gla_b_gate_f, gla_w_gate_b,
           gla_b_gate_b, gla_onorm, gla_w_out, swa_w_in, swa_sinks, swa_w_out):
    w1r, w2, gla, swa = _prep_weights(ffn_w1, ffn_w2, gla_w_in, gla_w_gate_f, gla_b_gate_f, gla_w_gate_b,
                                      gla_b_gate_b, gla_onorm, gla_w_out, swa_w_in, swa_sinks, swa_w_out)
    y_prompt = _trunk(x_prompt, norm_g, w1r, w2, gla, swa)
    y_sample = _trunk(x_sample, norm_g, w1r, w2, gla, swa)
    return (y_prompt, y_sample)
```

```python
import functools

import jax
import jax.numpy as jnp
from jax import lax
from jax.experimental import pallas as pl
from jax.experimental.pallas import tpu as pltpu

F32 = jnp.float32
BF16 = jnp.bfloat16

D_MODEL = 1024
DEPTH = 4
NORM_EPS = 1e-6

D_FF = 2816
FFN_RES = 0.5
FFN_CHUNK = 256
FFN_NCHUNK = D_FF // FFN_CHUNK
FFN_TILE = 1024
FFN_SUB = 512
GLA_HEADS = 4
GLA_DK = 128
GLA_DV = 256
GLA_QK = GLA_HEADS * GLA_DK
GLA_V = GLA_HEADS * GLA_DV
GLA_GATE_RANK = 16
GLA_TAU = 16.0
GLA_C = 128
GLA_MID = GLA_C // 2 - 1
GLA_NDEC = 3

SWA_Q_HEADS = 16
SWA_KV_HEADS = 4
SWA_GROUP = SWA_Q_HEADS // SWA_KV_HEADS
SWA_HD = 64
SWA_BLOCK = 128
SWA_Q = SWA_Q_HEADS * SWA_HD
SWA_KV = SWA_KV_HEADS * SWA_HD
ROPE_THETA = 500000.0
ROPE_DIM = SWA_HD // 4
ROPE_HALF = ROPE_DIM // 2
NEG_BIG = -1e30
LOG2E = 1.4426950408889634
SWA_QSCALE = SWA_HD ** -0.5 * LOG2E

LANES = 128
SWA_BD = 2 * LANES * SWA_KV_HEADS
SWA_PROJ_TILE = 1024
SWA_PROJ_SUB = 256
SWA_PROJ_COLS = 256
SWA_QB = 4
SWA_ROWS = 128
GLA_PROJ_TILE = 1024
GLA_PROJ_SUB = 256
GLA_PROJ_COLS = 256
GLA_REC_TILE = 1024
VMEM_LIMIT = 56 * 1024 * 1024


def _rms(x, g):
    ms = jnp.mean(x * x, axis=-1, keepdims=True)
    return x * lax.rsqrt(ms + NORM_EPS) * g


def _silu(x):
    return x * (1.0 / (1.0 + jnp.exp(-x)))


def _run_interleaved(first, second):
    order = [((i + 0.5) / len(first), 0, i) for i in range(len(first))]
    order += [((j + 0.5) / len(second), 1, j) for j in range(len(second))]
    for _, which, idx in sorted(order):
        (first, second)[which][idx]()


def _const_spec(shape):
    nd = len(shape)
    return pl.BlockSpec(shape, lambda *_: (0,) * nd, pipeline_mode=pl.Buffered(1))


def _params(sem, flags=None):
    return pltpu.CompilerParams(dimension_semantics=sem, vmem_limit_bytes=VMEM_LIMIT, flags=flags)


def _gla_tail(of_ref, ob_ref, gate_ref, x_ref, gon_ref, gmix_ref, wmix_ref, rows):
    parts = []
    for head in range(GLA_HEADS):
        lanes = slice(head * GLA_DV, (head + 1) * GLA_DV)
        oh = of_ref[rows, lanes].astype(F32) + ob_ref[rows, lanes].astype(F32)
        ms = jnp.mean(oh * oh, axis=-1, keepdims=True)
        on = oh * lax.rsqrt(ms + NORM_EPS) * gon_ref[:, lanes]
        parts.append((on * gate_ref[rows, lanes].astype(F32)).astype(BF16))
    y = jnp.dot(jnp.concatenate(parts, axis=1), wmix_ref[...], preferred_element_type=F32)
    return x_ref[rows, :] + _rms(y, gmix_ref[...])


def _ffn_body(*refs, after_gla):
    if after_gla:
        of_ref, ob_ref, gate_ref, x_ref, gon_ref, gmix_ref, wmix_ref, g_ref, w1_ref, w2_ref, o_ref, h_ref = refs
    else:
        x_ref, g_ref, w1_ref, w2_ref, o_ref, h_ref = refs
    nsub = x_ref.shape[0] // FFN_SUB
    subs = [slice(s * FFN_SUB, (s + 1) * FFN_SUB) for s in range(nsub)]
    xns = []
    for rows in subs:
        if after_gla:
            x1 = _gla_tail(of_ref, ob_ref, gate_ref, x_ref, gon_ref, gmix_ref, wmix_ref, rows)
            o_ref[rows, :] = x1
        else:
            x1 = x_ref[rows, :]
        xns.append(_rms(x1, g_ref[0:1, :]).astype(BF16))
    ys = []
    for xn in xns:
        for c in range(FFN_NCHUNK):
            cols = slice(c * FFN_CHUNK, (c + 1) * FFN_CHUNK)
            gate = jnp.dot(xn, w1_ref[:, cols], preferred_element_type=F32)
            up = jnp.dot(xn, w1_ref[:, D_FF + c * FFN_CHUNK:D_FF + (c + 1) * FFN_CHUNK],
                         preferred_element_type=F32)
            h_ref[:, cols] = (_silu(gate) * up).astype(BF16)
        ys.append(jnp.dot(h_ref[...], w2_ref[...], preferred_element_type=F32))
    base_ref = o_ref if after_gla else x_ref
    for rows, y in zip(subs, ys):
        o_ref[rows, :] = base_ref[rows, :] + FFN_RES * _rms(y, g_ref[1:2, :])


def _ffn(x2, g2, w1r, w2, layer, which, gla=None):
    t = x2.shape[0]
    tm = FFN_TILE
    pick = lambda shape: pl.BlockSpec((None, None) + shape, lambda i: (layer, which, 0, 0),
                                      pipeline_mode=pl.Buffered(1))
    tok = lambda: pl.BlockSpec((tm, D_MODEL), lambda i: (i, 0))
    ffn_specs = [_const_spec((2, D_MODEL)), pick((D_MODEL, 2 * D_FF)), pick((D_FF, D_MODEL))]
    if gla is None:
        operands, in_specs = (x2, g2, w1r, w2), [tok()] + ffn_specs
    else:
        o_f, o_b, gate, g_onorm, g_post, w_out = gla
        operands = (o_f, o_b, gate, x2, g_onorm, g_post, w_out, g2, w1r, w2)
        in_specs = ([tok(), tok(), tok(), tok(), _const_spec((1, GLA_V)), _const_spec((1, D_MODEL)),
                     _const_spec((GLA_V, D_MODEL))] + ffn_specs)
    return pl.pallas_call(
        functools.partial(_ffn_body, after_gla=gla is not None),
        grid=(t // tm,),
        in_specs=in_specs,
        out_specs=tok(),
        out_shape=jax.ShapeDtypeStruct((t, D_MODEL), F32),
        scratch_shapes=[pltpu.VMEM((FFN_SUB, D_FF), BF16)],
        compiler_params=_params(("parallel",)),
        name="ffn_gla" if gla is not None else "ffn",
    )(*operands)


def _gla_proj_body(x_ref, g_ref, wm_ref, wgd_ref, wgate_ref, bgate_ref,
                   qkf_ref, qkb_ref, v_ref, r_ref, dl_ref):
    tm = x_ref.shape[0]
    sub = GLA_PROJ_SUB
    row = lax.broadcasted_iota(jnp.int32, (GLA_C, GLA_C), 0)
    col = lax.broadcasted_iota(jnp.int32, (GLA_C, GLA_C), 1)
    tri_f = (col <= row).astype(BF16)
    tri_b = (col >= row).astype(BF16)

    ncol = (2 * GLA_QK + 2 * GLA_V) // GLA_PROJ_COLS

    def project_pieces(s, out):
        rows = slice(s * sub, (s + 1) * sub)

        def first():
            out["xn"] = _rms(x_ref[rows, :], g_ref[...]).astype(BF16)
            gd = jnp.dot(out["xn"], wgd_ref[...], preferred_element_type=F32).astype(BF16)
            out["z"] = jnp.dot(gd, wgate_ref[...], preferred_element_type=F32) + bgate_ref[...]

        def chunk(j):
            def run():
                cols = slice(j * GLA_PROJ_COLS, (j + 1) * GLA_PROJ_COLS)
                out["h"][j] = jnp.dot(out["xn"], wm_ref[:, cols], preferred_element_type=F32)
            return run

        out["h"] = [None] * ncol
        return [first] + [chunk(j) for j in range(ncol)]

    def finish_pieces(s, src):
        rows = slice(s * sub, (s + 1) * sub)
        hcols = lambda lo, hi: jnp.concatenate(src["h"][lo // GLA_PROJ_COLS:hi // GLA_PROJ_COLS], axis=1)

        def cast_v():
            v_ref[rows, :] = hcols(2 * GLA_QK, 2 * GLA_QK + GLA_V).astype(BF16)

        def cast_r():
            r_ref[rows, :] = _silu(hcols(2 * GLA_QK + GLA_V, 2 * GLA_QK + 2 * GLA_V)).astype(BF16)

        def unit(cc, d):
            def run():
                tri, out_ref, last, mid = ((tri_f, qkf_ref, GLA_C - 1, GLA_MID),
                                           (tri_b, qkb_ref, 0, GLA_MID + 1))[d]
                c = s * (sub // GLA_C) + cc
                crow = slice(cc * GLA_C, (cc + 1) * GLA_C)
                orow = slice(c * GLA_C, (c + 1) * GLA_C)
                q = hcols(0, GLA_QK)[crow, :] * (GLA_DK ** -0.5)
                k = hcols(GLA_QK, 2 * GLA_QK)[crow, :]
                z = src["z"][crow, d * GLA_QK:(d + 1) * GLA_QK]
                g = (jnp.minimum(z, 0.0) - jnp.log(1.0 + jnp.exp(-jnp.abs(z)))) * (LOG2E / GLA_TAU)
                g_hi = g.astype(BF16)
                g_lo = (g - g_hi.astype(F32)).astype(BF16)
                b = (jnp.dot(tri, g_hi, preferred_element_type=F32)
                     + jnp.dot(tri, g_lo, preferred_element_type=F32))
                b_mid = b[mid:mid + 1, :]
                b_last = b[last:last + 1, :]
                out_ref[orow, 0 * GLA_QK:1 * GLA_QK] = (q * jnp.exp2(b - b_mid)).astype(BF16)
                out_ref[orow, 1 * GLA_QK:2 * GLA_QK] = (k * jnp.exp2(b_mid - b)).astype(BF16)
                dcols = slice(d * GLA_QK, (d + 1) * GLA_QK)
                dl_ref[c, 0:1, dcols] = jnp.exp2(b_last)
                dl_ref[c, 1:2, dcols] = jnp.exp2(b_mid)
                dl_ref[c, 2:3, dcols] = jnp.exp2(b_last - b_mid)
            return run

        units = [unit(cc, d) for cc in range(sub // GLA_C) for d in range(2)]
        return units[:2] + [cast_v] + units[2:] + [cast_r]

    cur = {}
    _run_interleaved(project_pieces(0, cur), [])
    for s in range(tm // sub):
        nxt = {}
        _run_interleaved(project_pieces(s + 1, nxt) if s + 1 < tm // sub else [], finish_pieces(s, cur))
        cur = nxt


def _gla_proj(x2, g_pre, wm, wgd, wgate, bgate):
    t = x2.shape[0]
    tm = GLA_PROJ_TILE
    nc = tm // GLA_C
    tok = lambda w: pl.BlockSpec((tm, w), lambda i: (i, 0))
    return pl.pallas_call(
        _gla_proj_body,
        grid=(t // tm,),
        in_specs=[
            tok(D_MODEL),
            _const_spec((1, D_MODEL)),
            _const_spec((D_MODEL, 2 * GLA_QK + 2 * GLA_V)),
            _const_spec((D_MODEL, LANES)),
            _const_spec((LANES, 2 * GLA_QK)),
            _const_spec((1, 2 * GLA_QK)),
        ],
        out_specs=[
            tok(2 * GLA_QK), tok(2 * GLA_QK), tok(GLA_V), tok(GLA_V),
            pl.BlockSpec((nc, GLA_NDEC, 2 * GLA_QK), lambda i: (i, 0, 0)),
        ],
        out_shape=[
            jax.ShapeDtypeStruct((t, 2 * GLA_QK), BF16),
            jax.ShapeDtypeStruct((t, 2 * GLA_QK), BF16),
            jax.ShapeDtypeStruct((t, GLA_V), BF16),
            jax.ShapeDtypeStruct((t, GLA_V), BF16),
            jax.ShapeDtypeStruct((t // GLA_C, GLA_NDEC, 2 * GLA_QK), F32),
        ],
        compiler_params=_params(("parallel",)),
        name="gla_proj",
    )(x2, g_pre, wm, wgd, wgate, bgate)


def _gla_rec_body(qkf_ref, qkb_ref, vf_ref, vb_ref, dlf_ref, dlb_ref, of_ref, ob_ref, s_ref, a_ref, kv_ref):
    tb = qkf_ref.shape[1]
    nck = tb // GLA_C

    @pl.when(pl.program_id(1) == 0)
    def _():
        s_ref[...] = jnp.zeros_like(s_ref)

    row = lax.broadcasted_iota(jnp.int32, (GLA_C, GLA_C), 0)
    col = lax.broadcasted_iota(jnp.int32, (GLA_C, GLA_C), 1)
    dirs = ((qkf_ref, vf_ref, dlf_ref, of_ref, col <= row), (qkb_ref, vb_ref, dlb_ref, ob_ref, col >= row))
    lanes = lambda part, head: slice(part * GLA_QK + head * GLA_DK, part * GLA_QK + (head + 1) * GLA_DK)
    vlanes = lambda head: slice(head * GLA_DV, (head + 1) * GLA_DV)
    unit = lambda d, c, head: (d * nck + c) * GLA_HEADS + head

    for d, (qk_ref, v_ref, dl_ref, _, mask) in enumerate(dirs):
        for c in range(nck):
            rows = slice(c * GLA_C, (c + 1) * GLA_C)
            for head in range(GLA_HEADS):
                k_mid = qk_ref[0, rows, lanes(1, head)]
                scores = lax.dot_general(qk_ref[0, rows, lanes(0, head)], k_mid,
                                         (((1,), (1,)), ((), ())), preferred_element_type=F32)
                a_ref[unit(d, c, head)] = jnp.where(mask, scores, 0.0).astype(BF16)
                k_end = (k_mid.astype(F32) * dl_ref[0, c, 2:3, lanes(d, head)]).astype(BF16)
                kv_ref[unit(d, c, head)] = lax.dot_general(
                    k_end, v_ref[0, rows, vlanes(head)],
                    (((0,), (0,)), ((), ())), preferred_element_type=F32)

    for step in range(nck):
        for d, (qk_ref, v_ref, dl_ref, o_ref, _) in enumerate(dirs):
            c = step if d == 0 else nck - 1 - step
            rows = slice(c * GLA_C, (c + 1) * GLA_C)
            for head in range(GLA_HEADS):
                state = s_ref[d * GLA_HEADS + head]
                q_dec = (qk_ref[0, rows, lanes(0, head)].astype(F32)
                         * dl_ref[0, c, 1:2, lanes(d, head)]).astype(BF16)
                lhs = jnp.concatenate([a_ref[unit(d, c, head)], q_dec], axis=1)
                rhs = jnp.concatenate([v_ref[0, rows, vlanes(head)], state.astype(BF16)], axis=0)
                o_ref[0, rows, vlanes(head)] = jnp.dot(lhs, rhs, preferred_element_type=F32).astype(BF16)
                dl_row = dl_ref[0, c, 0:1, lanes(d, head)]
                dl_col = jnp.transpose(jnp.broadcast_to(dl_row, (GLA_DK, GLA_DK)))
                s_ref[d * GLA_HEADS + head] = (jnp.concatenate([dl_col, dl_col], axis=1) * state
                                               + kv_ref[unit(d, c, head)])


def _gla_rec(qkf, qkb, v, dl, bsz, seq):
    tb = GLA_REC_TILE
    nb = seq // tb
    nck = tb // GLA_C
    fwd = lambda b, i: (b, i, 0)
    bwd = lambda b, i: (b, nb - 1 - i, 0)
    return pl.pallas_call(
        _gla_rec_body,
        grid=(bsz, nb),
        in_specs=[
            pl.BlockSpec((1, tb, 2 * GLA_QK), fwd),
            pl.BlockSpec((1, tb, 2 * GLA_QK), bwd),
            pl.BlockSpec((1, tb, GLA_V), fwd),
            pl.BlockSpec((1, tb, GLA_V), bwd),
            pl.BlockSpec((1, nck, GLA_NDEC, 2 * GLA_QK), lambda b, i: (b, i, 0, 0)),
            pl.BlockSpec((1, nck, GLA_NDEC, 2 * GLA_QK), lambda b, i: (b, nb - 1 - i, 0, 0)),
        ],
        out_specs=[
            pl.BlockSpec((1, tb, GLA_V), fwd),
            pl.BlockSpec((1, tb, GLA_V), bwd),
        ],
        out_shape=[
            jax.ShapeDtypeStruct((bsz, seq, GLA_V), BF16),
            jax.ShapeDtypeStruct((bsz, seq, GLA_V), BF16),
        ],
        scratch_shapes=[
            pltpu.VMEM((2 * GLA_HEADS, GLA_DK, GLA_DV), F32),
            pltpu.VMEM((2 * nck * GLA_HEADS, GLA_C, GLA_C), BF16),
            pltpu.VMEM((2 * nck * GLA_HEADS, GLA_DK, GLA_DV), F32),
        ],
        compiler_params=_params(("parallel", "arbitrary")),
        name="gla_rec",
    )(qkf, qkb, v, v, dl, dl)


def _gla_mixer(x, g_pre, w):
    bsz, seq, _ = x.shape
    x2 = x.reshape(bsz * seq, D_MODEL)
    qkf, qkb, v, gate, dl = _gla_proj(x2, g_pre, w["wm"], w["wgd"], w["wgate"], w["bgate"])
    shp = lambda a: a.reshape(bsz, seq, a.shape[-1])
    o_f, o_b = _gla_rec(shp(qkf), shp(qkb), shp(v), dl.reshape(bsz, seq // GLA_C, GLA_NDEC, 2 * GLA_QK), bsz, seq)
    return o_f.reshape(bsz * seq, GLA_V), o_b.reshape(bsz * seq, GLA_V), gate


def _rope(z, cos, sin_a, sin_b):
    return (z * cos + pltpu.roll(z, LANES - ROPE_HALF, axis=1) * sin_a
            + pltpu.roll(z, ROPE_HALF, axis=1) * sin_b)


def _swa_proj_body(x_ref, g_ref, w_ref, cos_ref, sa_ref, sb_ref, q_ref, kbd_ref, vbd_ref):
    sub = SWA_PROJ_SUB
    nsub = x_ref.shape[1] // sub
    ncol = (SWA_Q + 2 * SWA_KV) // SWA_PROJ_COLS
    per = SWA_PROJ_COLS // LANES
    low = lax.broadcasted_iota(jnp.int32, (sub, LANES), 1) < SWA_HD

    def project_pieces(s, out):
        rows = slice(s * sub, (s + 1) * sub)

        def first():
            out["xn"] = _rms(x_ref[0, rows, :], g_ref[...]).astype(BF16)

        def chunk(j):
            def run():
                cols = slice(j * SWA_PROJ_COLS, (j + 1) * SWA_PROJ_COLS)
                out["h"][j] = jnp.dot(out["xn"], w_ref[:, cols], preferred_element_type=F32)
            return run

        out["h"] = [None] * ncol
        return [first] + [chunk(j) for j in range(ncol)]

    def finish_pieces(s, src):
        rows = slice(s * sub, (s + 1) * sub)
        hblk = lambda b: src["h"][b // per][:, (b % per) * LANES:(b % per + 1) * LANES]
        rope = lambda z: _rope(z, cos_ref[rows, :], sa_ref[rows, :], sb_ref[rows, :])

        def q_piece(j):
            def run():
                q_ref[0, rows, j * LANES:(j + 1) * LANES] = (rope(hblk(j)) * SWA_QSCALE).astype(BF16)
            return run

        def spread(z, out_ref, j):
            zr = pltpu.roll(z, SWA_HD, axis=1)
            parts = (jnp.where(low, z, 0.0), jnp.where(low, 0.0, zr), jnp.where(low, zr, 0.0),
                     jnp.where(low, 0.0, z))
            for i, part in enumerate(parts):
                out_ref[0, rows, (4 * j + i) * LANES:(4 * j + i + 1) * LANES] = part.astype(BF16)

        def k_piece(j):
            return lambda: spread(rope(hblk(SWA_Q // LANES + j)), kbd_ref, j)

        def v_piece(j):
            return lambda: spread(hblk((SWA_Q + SWA_KV) // LANES + j), vbd_ref, j)

        return ([q_piece(j) for j in range(SWA_Q // LANES)] + [k_piece(j) for j in range(SWA_KV // LANES)]
                + [v_piece(j) for j in range(SWA_KV // LANES)])

    cur = {}
    _run_interleaved(project_pieces(0, cur), [])
    for s in range(nsub):
        nxt = {}
        _run_interleaved(project_pieces(s + 1, nxt) if s + 1 < nsub else [], finish_pieces(s, cur))
        cur = nxt


def _swa_proj(x, g_pre, w_in, cos, sin_a, sin_b):
    bsz, seq, _ = x.shape
    tm = SWA_PROJ_TILE
    tok = lambda w: pl.BlockSpec((1, tm, w), lambda b, i: (b, i, 0))
    tab = lambda: pl.BlockSpec((tm, LANES), lambda b, i: (i, 0))
    return pl.pallas_call(
        _swa_proj_body,
        grid=(bsz, seq // tm),
        in_specs=[tok(D_MODEL), _const_spec((1, D_MODEL)), _const_spec((D_MODEL, SWA_Q + 2 * SWA_KV)),
                  tab(), tab(), tab()],
        out_specs=[tok(SWA_Q), tok(SWA_BD), tok(SWA_BD)],
        out_shape=[
            jax.ShapeDtypeStruct((bsz, seq, SWA_Q), BF16),
            jax.ShapeDtypeStruct((bsz, seq, SWA_BD), BF16),
            jax.ShapeDtypeStruct((bsz, seq, SWA_BD), BF16),
        ],
        compiler_params=_params(("parallel", "parallel")),
        name="swa_proj",
    )(x, g_pre, w_in, cos, sin_a, sin_b)


def _swa_attn_body(nq, sink_ref, q_ref, kp_ref, km_ref, kn_ref, vp_ref, vm_ref, vn_ref,
                   x_ref, gpost_ref, wout_ref, y_ref, s_ref, p_ref, o_ref, inv_ref, bias_ref):
    qb = SWA_QB
    n = pl.program_id(1)
    blk = SWA_BLOCK
    row = lax.broadcasted_iota(jnp.int32, (blk, 2 * LANES), 0)
    col = lax.broadcasted_iota(jnp.int32, (blk, 2 * LANES), 1) & (LANES - 1)
    bias_ref[0] = jnp.where(col >= row, 0.0, NEG_BIG).astype(BF16)
    bias_ref[1] = jnp.where(col <= row, 0.0, NEG_BIG).astype(BF16)
    bias_ref[2] = jnp.full((blk, 2 * LANES), NEG_BIG, BF16)
    low = lax.broadcasted_iota(jnp.int32, (SWA_ROWS, LANES), 1) < SWA_HD
    head_a = lax.broadcasted_iota(jnp.int32, (SWA_ROWS, 2 * LANES), 1) < LANES

    def key_block(prev_ref, main_ref, next_ref, j, lanes):
        if j == 0:
            return prev_ref[0, :, lanes]
        if j == qb + 1:
            return next_ref[0, :, lanes]
        return main_ref[0, (j - 1) * blk:j * blk, lanes]

    def head_lanes(hk):
        bd = slice(2 * hk * LANES, (2 * hk + 2) * LANES)
        pair_lanes = [slice((2 * hk + e) * LANES, (2 * hk + e + 1) * LANES) for e in range(2)]
        return bd, pair_lanes

    def score_pieces(hk):
        bd, pair_lanes = head_lanes(hk)
        buf = hk % 2

        def piece(j):
            def run():
                i0, i1 = max(j - 2, 0), min(j, qb - 1)
                kj = key_block(kp_ref, km_ref, kn_ref, j, bd)
                rhs = jnp.concatenate([kj[:, :LANES], kj[:, LANES:]], axis=0)
                lhs = jnp.concatenate([q_ref[0, i0 * blk:(i1 + 1) * blk, pl_] for pl_ in pair_lanes], axis=0)
                s = lax.dot_general(lhs, rhs, (((1,), (1,)), ((), ())), preferred_element_type=F32)
                nrow = (i1 - i0 + 1) * blk
                for e in range(2):
                    for i in range(i0, i1 + 1):
                        r0 = e * nrow + (i - i0) * blk
                        s_ref[buf, e, i, j - i] = s[r0:r0 + blk, :].astype(BF16)
            return run

        return [piece(j) for j in range(qb + 2)]

    def softmax_pieces(hk):
        _, pair_lanes = head_lanes(hk)
        buf = hk % 2

        def piece(e, i):
            def run():
                sink_a = sink_ref[SWA_GROUP * hk + 2 * e] * LOG2E
                sink_b = sink_ref[SWA_GROUP * hk + 2 * e + 1] * LOG2E
                g = n * qb + i
                tbl_prev = 0 if i > 0 else jnp.where(g > 0, 0, 2)
                tbl_next = 1 if i < qb - 1 else jnp.where(g < nq - 1, 1, 2)
                for r0 in range(0, blk, SWA_ROWS):
                    rr = slice(r0, r0 + SWA_ROWS)
                    sp = s_ref[buf, e, i, 0, rr, :] + bias_ref[tbl_prev, rr, :]
                    sc = s_ref[buf, e, i, 1, rr, :]
                    sn = s_ref[buf, e, i, 2, rr, :] + bias_ref[tbl_next, rr, :]
                    mx = jnp.maximum(jnp.maximum(sp, sc), sn).astype(F32)
                    m_a = jnp.maximum(jnp.max(mx[:, :LANES], axis=-1, keepdims=True), sink_a)
                    m_b = jnp.maximum(jnp.max(mx[:, LANES:], axis=-1, keepdims=True), sink_b)
                    m = jnp.where(head_a, m_a, m_b).astype(BF16)
                    probs = [jnp.exp2(sw - m) for sw in (sp, sc, sn)]
                    tot = (probs[0] + probs[1] + probs[2]).astype(F32)
                    m32 = m.astype(F32)
                    inv_a = 1.0 / (jnp.sum(tot[:, :LANES], axis=-1, keepdims=True)
                                   + jnp.exp2(sink_a - m32[:, :1]))
                    inv_b = 1.0 / (jnp.sum(tot[:, LANES:], axis=-1, keepdims=True)
                                   + jnp.exp2(sink_b - m32[:, LANES:LANES + 1]))
                    for which, pw in enumerate(probs):
                        slot = 2 - which
                        p_ref[buf, e, i + which, slot * blk + r0:slot * blk + r0 + SWA_ROWS, :] = pw
                    inv_ref[i * blk + r0:i * blk + r0 + SWA_ROWS, pair_lanes[e]] = jnp.where(low, inv_a, inv_b)
            return run

        return [piece(e, i) for e in range(2) for i in range(qb)]

    def value_pieces(hk):
        bd, pair_lanes = head_lanes(hk)
        buf = hk % 2

        def piece(j):
            def run():
                i0, i1 = max(j - 2, 0), min(j, qb - 1)
                vj = key_block(vp_ref, vm_ref, vn_ref, j, bd)
                rhs = jnp.concatenate([vj[:, :LANES], vj[:, LANES:]], axis=0)
                r0, r1 = (i0 - (j - 2)) * blk, (i1 - (j - 2) + 1) * blk
                lhs = jnp.concatenate([p_ref[buf, e, j, r0:r1, :] for e in range(2)], axis=0)
                o = jnp.dot(lhs, rhs, preferred_element_type=F32)
                nrow = r1 - r0
                for e in range(2):
                    for i in range(i0, i1 + 1):
                        part = o[e * nrow + (i - i0) * blk:e * nrow + (i - i0 + 1) * blk, :]
                        rows = slice(i * blk, (i + 1) * blk)
                        if j == i:
                            o_ref[rows, pair_lanes[e]] = part
                        else:
                            o_ref[rows, pair_lanes[e]] += part
            return run

        return [piece(j) for j in range(qb + 2)]

    for t in range(SWA_KV_HEADS + 2):
        mm = []
        if t < SWA_KV_HEADS:
            mm += score_pieces(t)
        if 0 <= t - 2 < SWA_KV_HEADS:
            mm += value_pieces(t - 2)
        ew = softmax_pieces(t - 1) if 0 <= t - 1 < SWA_KV_HEADS else []
        _run_interleaved(mm, ew)
    o_all = (o_ref[...] * inv_ref[...]).astype(BF16)
    y = jnp.dot(o_all, wout_ref[...], preferred_element_type=F32)
    y_ref[0] = x_ref[0] + _rms(y, gpost_ref[...])


def _swa_attn(q, k, v, x, sinks, g_post, w_out):
    bsz, seq, _ = x.shape
    nq = seq // SWA_BLOCK
    qb = SWA_QB
    tq = qb * SWA_BLOCK
    cur = lambda b, n: (b, n, 0)
    prev = lambda b, n: (b, jnp.maximum(n * qb - 1, 0), 0)
    nxt = lambda b, n: (b, jnp.minimum((n + 1) * qb, nq - 1), 0)
    halo = lambda im: pl.BlockSpec((1, SWA_BLOCK, SWA_BD), im)
    main = lambda w: pl.BlockSpec((1, tq, w), cur)
    return pl.pallas_call(
        functools.partial(_swa_attn_body, nq),
        grid=(bsz, nq // qb),
        in_specs=[
            pl.BlockSpec(memory_space=pltpu.SMEM),
            main(SWA_Q),
            halo(prev), main(SWA_BD), halo(nxt), halo(prev), main(SWA_BD), halo(nxt),
            main(D_MODEL),
            _const_spec((1, D_MODEL)),
            _const_spec((SWA_Q, D_MODEL)),
        ],
        out_specs=main(D_MODEL),
        out_shape=jax.ShapeDtypeStruct((bsz, seq, D_MODEL), F32),
        scratch_shapes=[
            pltpu.VMEM((2, 2, qb, 3, SWA_BLOCK, 2 * LANES), BF16),
            pltpu.VMEM((2, 2, qb + 2, 3 * SWA_BLOCK, 2 * LANES), BF16),
            pltpu.VMEM((tq, SWA_Q), F32),
            pltpu.VMEM((tq, SWA_Q), F32),
            pltpu.VMEM((3, SWA_BLOCK, 2 * LANES), BF16),
        ],
        compiler_params=_params(("parallel", "parallel")),
        name="swa_attn",
    )(sinks, q, k, k, k, v, v, v, x, g_post, w_out)


def _rope_tables(seq):
    inv_freq = ROPE_THETA ** (-(jnp.arange(ROPE_HALF, dtype=F32) * 2.0 / ROPE_DIM))
    ang = jnp.arange(seq, dtype=F32)[:, None] * inv_freq[None, :]
    cos8, sin8 = jnp.cos(ang), jnp.sin(ang)
    ones = jnp.ones((seq, SWA_HD - ROPE_DIM), F32)
    zeros8 = jnp.zeros((seq, ROPE_HALF), F32)
    zeros = jnp.zeros((seq, SWA_HD - ROPE_DIM), F32)
    cos = jnp.concatenate([cos8, cos8, ones], axis=1)
    sin_a = jnp.concatenate([-sin8, zeros8, zeros], axis=1)
    sin_b = jnp.concatenate([zeros8, sin8, zeros], axis=1)
    rep = LANES // SWA_HD
    return tuple(jnp.tile(t, (1, rep)) for t in (cos, sin_a, sin_b))


def _swa_layer(x, g_pre, g_post, w):
    seq = x.shape[1]
    cos, sin_a, sin_b = _rope_tables(seq)
    q, k, v = _swa_proj(x, g_pre, w["win"], cos, sin_a, sin_b)
    return _swa_attn(q, k, v, x, w["sinks"], g_post, w["wout"])


def _ffn_layer(x, g2, w1r, w2, layer, which, gla=None):
    bsz, seq, _ = x.shape
    return _ffn(x.reshape(bsz * seq, D_MODEL), g2, w1r, w2, layer, which, gla).reshape(bsz, seq, D_MODEL)


def _prep_weights(ffn_w1, ffn_w2, gla_w_in, gla_w_gate_f, gla_b_gate_f, gla_w_gate_b, gla_b_gate_b,
                  gla_onorm, gla_w_out, swa_w_in, swa_sinks, swa_w_out):
    w1r = ffn_w1.astype(BF16)
    w2 = ffn_w2.astype(BF16)
    gla = []
    for j in range(gla_w_in.shape[0]):
        n_main = 2 * GLA_QK + 2 * GLA_V
        wgd = jnp.zeros((D_MODEL, LANES), F32).at[:, :2 * GLA_GATE_RANK].set(gla_w_in[j][:, n_main:])
        wgate = jnp.zeros((LANES, 2 * GLA_QK), F32)
        wgate = wgate.at[:GLA_GATE_RANK, :GLA_QK].set(gla_w_gate_f[j])
        wgate = wgate.at[GLA_GATE_RANK:2 * GLA_GATE_RANK, GLA_QK:].set(gla_w_gate_b[j])
        gla.append(dict(
            wm=gla_w_in[j][:, :n_main].astype(BF16),
            wgd=wgd.astype(BF16),
            wgate=wgate.astype(BF16),
            bgate=jnp.concatenate([gla_b_gate_f[j], gla_b_gate_b[j]])[None, :],
            onorm=gla_onorm[j][None, :],
            wout=gla_w_out[j].astype(BF16),
        ))
    swa = []
    for j in range(swa_w_in.shape[0]):
        swa.append(dict(win=swa_w_in[j].astype(BF16), sinks=swa_sinks[j], wout=swa_w_out[j].astype(BF16)))
    return w1r, w2, gla, swa


def _trunk(x, norm_g, w1r, w2, gla, swa):
    for i in range(DEPTH):
        g = norm_g[i]
        x = _ffn_layer(x, g[0:2], w1r, w2, i, 0)
        if i % 2 == 0:
            w = gla[i // 2]
            o_f, o_b, gate = _gla_mixer(x, g[2:3], w)
            x = _ffn_layer(x, g[4:6], w1r, w2, i, 1, gla=(o_f, o_b, gate, w["onorm"], g[3:4], w["wout"]))
        else:
            x = _swa_layer(x, g[2:3], g[3:4], swa[i // 2])
            x = _ffn_layer(x, g[4:6], w1r, w2, i, 1)
    return x


def kernel(x_prompt, x_sample, norm_g, ffn_w1, ffn_w2, gla_w_in, gla_w_gate_f, gla_b_gate_f, gla_w_gate_b,
           gla_b_gate_b, gla_onorm, gla_w_out, swa_w_in, swa_sinks, swa_w_out):
    w1r, w2, gla, swa = _prep_weights(ffn_w1, ffn_w2, gla_w_in, gla_w_gate_f, gla_b_gate_f, gla_w_gate_b,
                                      gla_b_gate_b, gla_onorm, gla_w_out, swa_w_in, swa_sinks, swa_w_out)
    y_prompt = _trunk(x_prompt, norm_g, w1r, w2, gla, swa)
    y_sample = _trunk(x_sample, norm_g, w1r, w2, gla, swa)
    return (y_prompt, y_sample)
```

```python
import functools

import jax
import jax.numpy as jnp
from jax import lax
from jax.experimental import pallas as pl
from jax.experimental.pallas import tpu as pltpu

F32 = jnp.float32
BF16 = jnp.bfloat16

D_MODEL = 1024
DEPTH = 4
NORM_EPS = 1e-6

D_FF = 2816
FFN_RES = 0.5
FFN_CHUNK = 256
FFN_NCHUNK = D_FF // FFN_CHUNK
FFN_TILE = 1024
FFN_SUB = 512

GLA_HEADS = 4
GLA_DK = 128
GLA_DV = 256
GLA_QK = GLA_HEADS * GLA_DK
GLA_V = GLA_HEADS * GLA_DV
GLA_GATE_RANK = 16
GLA_TAU = 16.0
GLA_C = 128
GLA_MID = GLA_C // 2 - 1
GLA_NDEC = 3

SWA_Q_HEADS = 16
SWA_KV_HEADS = 4
SWA_GROUP = SWA_Q_HEADS // SWA_KV_HEADS
SWA_HD = 64
SWA_BLOCK = 128
SWA_Q = SWA_Q_HEADS * SWA_HD
SWA_KV = SWA_KV_HEADS * SWA_HD
ROPE_THETA = 500000.0
ROPE_DIM = SWA_HD // 4
ROPE_HALF = ROPE_DIM // 2
NEG_BIG = -1e30
LOG2E = 1.4426950408889634
SWA_QSCALE = SWA_HD ** -0.5 * LOG2E

LANES = 128
SWA_BD = 2 * LANES * SWA_KV_HEADS
SWA_PROJ_TILE = 1024
SWA_PROJ_SUB = 256
SWA_PROJ_COLS = 256
SWA_QB = 4
SWA_ROWS = 128
GLA_PROJ_TILE = 1024
GLA_PROJ_SUB = 256
GLA_PROJ_COLS = 256
GLA_REC_TILE = 1024
VMEM_LIMIT = 56 * 1024 * 1024


def _rms(x, g):
    ms = jnp.mean(x * x, axis=-1, keepdims=True)
    return x * lax.rsqrt(ms + NORM_EPS) * g


def _silu(x):
    return x * (1.0 / (1.0 + jnp.exp(-x)))


def _run_interleaved(first, second):
    order = [((i + 0.5) / len(first), 0, i) for i in range(len(first))]
    order += [((j + 0.5) / len(second), 1, j) for j in range(len(second))]
    for _, which, idx in sorted(order):
        (first, second)[which][idx]()


def _const_spec(shape):
    nd = len(shape)
    return pl.BlockSpec(shape, lambda *_: (0,) * nd, pipeline_mode=pl.Buffered(1))


def _params(sem):
    return pltpu.CompilerParams(dimension_semantics=sem, vmem_limit_bytes=VMEM_LIMIT)


def _gla_tail(of_ref, ob_ref, gate_ref, x_ref, gon_ref, gmix_ref, wmix_ref, rows):
    parts = []
    for head in range(GLA_HEADS):
        lanes = slice(head * GLA_DV, (head + 1) * GLA_DV)
        oh = of_ref[rows, lanes].astype(F32) + ob_ref[rows, lanes].astype(F32)
        ms = jnp.mean(oh * oh, axis=-1, keepdims=True)
        on = oh * lax.rsqrt(ms + NORM_EPS) * gon_ref[:, lanes]
        parts.append((on * gate_ref[rows, lanes].astype(F32)).astype(BF16))
    y = jnp.dot(jnp.concatenate(parts, axis=1), wmix_ref[...], preferred_element_type=F32)
    return x_ref[rows, :] + _rms(y, gmix_ref[...])


def _ffn_body(*refs, after_gla):
    if after_gla:
        of_ref, ob_ref, gate_ref, x_ref, gon_ref, gmix_ref, wmix_ref, g_ref, w1_ref, w2_ref, o_ref, h_ref = refs
    else:
        x_ref, g_ref, w1_ref, w2_ref, o_ref, h_ref = refs
    nsub = x_ref.shape[0] // FFN_SUB
    subs = [slice(s * FFN_SUB, (s + 1) * FFN_SUB) for s in range(nsub)]
    xns = []
    for rows in subs:
        if after_gla:
            x1 = _gla_tail(of_ref, ob_ref, gate_ref, x_ref, gon_ref, gmix_ref, wmix_ref, rows)
            o_ref[rows, :] = x1
        else:
            x1 = x_ref[rows, :]
        xns.append(_rms(x1, g_ref[0:1, :]).astype(BF16))
    ys = []
    for xn in xns:
        for c in range(FFN_NCHUNK):
            cols = slice(c * FFN_CHUNK, (c + 1) * FFN_CHUNK)
            gate = jnp.dot(xn, w1_ref[:, cols], preferred_element_type=F32)
            up = jnp.dot(xn, w1_ref[:, D_FF + c * FFN_CHUNK:D_FF + (c + 1) * FFN_CHUNK],
                         preferred_element_type=F32)
            h_ref[:, cols] = (_silu(gate) * up).astype(BF16)
        ys.append(jnp.dot(h_ref[...], w2_ref[...], preferred_element_type=F32))
    base_ref = o_ref if after_gla else x_ref
    for rows, y in zip(subs, ys):
        o_ref[rows, :] = base_ref[rows, :] + FFN_RES * _rms(y, g_ref[1:2, :])


def _ffn(x2, g2, w1r, w2, layer, which, gla=None):
    t = x2.shape[0]
    tm = FFN_TILE
    pick = lambda shape: pl.BlockSpec((None, None) + shape, lambda i: (layer, which, 0, 0),
                                      pipeline_mode=pl.Buffered(1))
    tok = lambda: pl.BlockSpec((tm, D_MODEL), lambda i: (i, 0))
    ffn_specs = [_const_spec((2, D_MODEL)), pick((D_MODEL, 2 * D_FF)), pick((D_FF, D_MODEL))]
    if gla is None:
        operands, in_specs = (x2, g2, w1r, w2), [tok()] + ffn_specs
    else:
        o_f, o_b, gate, g_onorm, g_post, w_out = gla
        operands = (o_f, o_b, gate, x2, g_onorm, g_post, w_out, g2, w1r, w2)
        in_specs = ([tok(), tok(), tok(), tok(), _const_spec((1, GLA_V)), _const_spec((1, D_MODEL)),
                     _const_spec((GLA_V, D_MODEL))] + ffn_specs)
    return pl.pallas_call(
        functools.partial(_ffn_body, after_gla=gla is not None),
        grid=(t // tm,),
        in_specs=in_specs,
        out_specs=tok(),
        out_shape=jax.ShapeDtypeStruct((t, D_MODEL), F32),
        scratch_shapes=[pltpu.VMEM((FFN_SUB, D_FF), BF16)],
        compiler_params=_params(("parallel",)),
        name="ffn_gla" if gla is not None else "ffn",
    )(*operands)


def _gla_proj_body(x_ref, g_ref, wm_ref, wgd_ref, wgate_ref, bgate_ref,
                   qkf_ref, qkb_ref, v_ref, r_ref, dl_ref):
    tm = x_ref.shape[0]
    sub = GLA_PROJ_SUB
    row = lax.broadcasted_iota(jnp.int32, (GLA_C, 2 * GLA_C), 0)
    col = lax.broadcasted_iota(jnp.int32, (GLA_C, 2 * GLA_C), 1) & (GLA_C - 1)
    tri_f = (col <= row).astype(BF16)
    tri_b = (col >= row).astype(BF16)

    ncol = (2 * GLA_QK + 2 * GLA_V) // GLA_PROJ_COLS

    def project_pieces(s, out):
        rows = slice(s * sub, (s + 1) * sub)

        def first():
            out["xn"] = _rms(x_ref[rows, :], g_ref[...]).astype(BF16)
            gd = jnp.dot(out["xn"], wgd_ref[...], preferred_element_type=F32).astype(BF16)
            out["z"] = jnp.dot(gd, wgate_ref[...], preferred_element_type=F32) + bgate_ref[...]

        def chunk(j):
            def run():
                cols = slice(j * GLA_PROJ_COLS, (j + 1) * GLA_PROJ_COLS)
                out["h"][j] = jnp.dot(out["xn"], wm_ref[:, cols], preferred_element_type=F32)
            return run

        out["h"] = [None] * ncol
        return [first] + [chunk(j) for j in range(ncol)]

    def finish_pieces(s, src):
        rows = slice(s * sub, (s + 1) * sub)
        hcols = lambda lo, hi: jnp.concatenate(src["h"][lo // GLA_PROJ_COLS:hi // GLA_PROJ_COLS], axis=1)

        def cast_v():
            v_ref[rows, :] = hcols(2 * GLA_QK, 2 * GLA_QK + GLA_V).astype(BF16)

        def cast_r():
            r_ref[rows, :] = _silu(hcols(2 * GLA_QK + GLA_V, 2 * GLA_QK + 2 * GLA_V)).astype(BF16)

        def unit(cc, d):
            def run():
                tri, out_ref, last, mid = ((tri_f, qkf_ref, GLA_C - 1, GLA_MID),
                                           (tri_b, qkb_ref, 0, GLA_MID + 1))[d]
                c = s * (sub // GLA_C) + cc
                crow = slice(cc * GLA_C, (cc + 1) * GLA_C)
                orow = slice(c * GLA_C, (c + 1) * GLA_C)
                q = hcols(0, GLA_QK)[crow, :] * (GLA_DK ** -0.5)
                k = hcols(GLA_QK, 2 * GLA_QK)[crow, :]
                z = src["z"][crow, d * GLA_QK:(d + 1) * GLA_QK]
                g = (jnp.minimum(z, 0.0) - jnp.log(1.0 + jnp.exp(-jnp.abs(z)))) * (LOG2E / GLA_TAU)
                g_hi = g.astype(BF16)
                g_lo = (g - g_hi.astype(F32)).astype(BF16)
                b = jnp.dot(tri, jnp.concatenate([g_hi, g_lo], axis=0),
                            preferred_element_type=F32)
                b_mid = b[mid:mid + 1, :]
                b_last = b[last:last + 1, :]
                out_ref[orow, 0 * GLA_QK:1 * GLA_QK] = (q * jnp.exp2(b - b_mid)).astype(BF16)
                out_ref[orow, 1 * GLA_QK:2 * GLA_QK] = (k * jnp.exp2(b_mid - b)).astype(BF16)
                dcols = slice(d * GLA_QK, (d + 1) * GLA_QK)
                dl_ref[c, 0:1, dcols] = jnp.exp2(b_last)
                dl_ref[c, 1:2, dcols] = jnp.exp2(b_mid)
                dl_ref[c, 2:3, dcols] = jnp.exp2(b_last - b_mid)
            return run

        units = [unit(cc, d) for cc in range(sub // GLA_C) for d in range(2)]
        return units[:2] + [cast_v] + units[2:] + [cast_r]

    cur = {}
    _run_interleaved(project_pieces(0, cur), [])
    for s in range(tm // sub):
        nxt = {}
        _run_interleaved(project_pieces(s + 1, nxt) if s + 1 < tm // sub else [], finish_pieces(s, cur))
        cur = nxt


def _gla_proj(x2, g_pre, wm, wgd, wgate, bgate):
    t = x2.shape[0]
    tm = GLA_PROJ_TILE
    nc = tm // GLA_C
    tok = lambda w: pl.BlockSpec((tm, w), lambda i: (i, 0))
    return pl.pallas_call(
        _gla_proj_body,
        grid=(t // tm,),
        in_specs=[
            tok(D_MODEL),
            _const_spec((1, D_MODEL)),
            _const_spec((D_MODEL, 2 * GLA_QK + 2 * GLA_V)),
            _const_spec((D_MODEL, LANES)),
            _const_spec((LANES, 2 * GLA_QK)),
            _const_spec((1, 2 * GLA_QK)),
        ],
        out_specs=[
            tok(2 * GLA_QK), tok(2 * GLA_QK), tok(GLA_V), tok(GLA_V),
            pl.BlockSpec((nc, GLA_NDEC, 2 * GLA_QK), lambda i: (i, 0, 0)),
        ],
        out_shape=[
            jax.ShapeDtypeStruct((t, 2 * GLA_QK), BF16),
            jax.ShapeDtypeStruct((t, 2 * GLA_QK), BF16),
            jax.ShapeDtypeStruct((t, GLA_V), BF16),
            jax.ShapeDtypeStruct((t, GLA_V), BF16),
            jax.ShapeDtypeStruct((t // GLA_C, GLA_NDEC, 2 * GLA_QK), F32),
        ],
        compiler_params=_params(("parallel",)),
        name="gla_proj",
    )(x2, g_pre, wm, wgd, wgate, bgate)


def _gla_rec_body(qkf_ref, qkb_ref, vf_ref, vb_ref, dlf_ref, dlb_ref, of_ref, ob_ref, s_ref, a_ref, kv_ref):
    tb = qkf_ref.shape[1]
    nck = tb // GLA_C

    @pl.when(pl.program_id(1) == 0)
    def _():
        s_ref[...] = jnp.zeros_like(s_ref)

    row = lax.broadcasted_iota(jnp.int32, (GLA_C, GLA_C), 0)
    col = lax.broadcasted_iota(jnp.int32, (GLA_C, GLA_C), 1)
    dirs = ((qkf_ref, vf_ref, dlf_ref, of_ref, col <= row), (qkb_ref, vb_ref, dlb_ref, ob_ref, col >= row))
    lanes = lambda part, head: slice(part * GLA_QK + head * GLA_DK, part * GLA_QK + (head + 1) * GLA_DK)
    vlanes = lambda head: slice(head * GLA_DV, (head + 1) * GLA_DV)
    unit = lambda d, c, head: (d * nck + c) * GLA_HEADS + head

    for d, (qk_ref, v_ref, dl_ref, _, mask) in enumerate(dirs):
        for c in range(nck):
            rows = slice(c * GLA_C, (c + 1) * GLA_C)
            for head in range(GLA_HEADS):
                k_mid = qk_ref[0, rows, lanes(1, head)]
                scores = lax.dot_general(qk_ref[0, rows, lanes(0, head)], k_mid,
                                         (((1,), (1,)), ((), ())), preferred_element_type=F32)
                a_ref[unit(d, c, head)] = jnp.where(mask, scores, 0.0).astype(BF16)
                k_end = (k_mid.astype(F32) * dl_ref[0, c, 2:3, lanes(d, head)]).astype(BF16)
                kv_ref[unit(d, c, head)] = lax.dot_general(
                    k_end, v_ref[0, rows, vlanes(head)],
                    (((0,), (0,)), ((), ())), preferred_element_type=F32)

    for step in range(nck):
        for d, (qk_ref, v_ref, dl_ref, o_ref, _) in enumerate(dirs):
            c = step if d == 0 else nck - 1 - step
            rows = slice(c * GLA_C, (c + 1) * GLA_C)
            for head in range(GLA_HEADS):
                state = s_ref[d * GLA_HEADS + head]
                q_dec = (qk_ref[0, rows, lanes(0, head)].astype(F32)
                         * dl_ref[0, c, 1:2, lanes(d, head)]).astype(BF16)
                lhs = jnp.concatenate([a_ref[unit(d, c, head)], q_dec], axis=1)
                rhs = jnp.concatenate([v_ref[0, rows, vlanes(head)], state.astype(BF16)], axis=0)
                o_ref[0, rows, vlanes(head)] = jnp.dot(lhs, rhs, preferred_element_type=F32).astype(BF16)
                dl_row = dl_ref[0, c, 0:1, lanes(d, head)]
                dl_col = jnp.transpose(jnp.broadcast_to(dl_row, (GLA_DK, GLA_DK)))
                s_ref[d * GLA_HEADS + head] = (jnp.concatenate([dl_col, dl_col], axis=1) * state
                                               + kv_ref[unit(d, c, head)])


def _gla_rec(qkf, qkb, v, dl, bsz, seq):
    tb = GLA_REC_TILE
    nb = seq // tb
    nck = tb // GLA_C
    fwd = lambda b, i: (b, i, 0)
    bwd = lambda b, i: (b, nb - 1 - i, 0)
    return pl.pallas_call(
        _gla_rec_body,
        grid=(bsz, nb),
        in_specs=[
            pl.BlockSpec((1, tb, 2 * GLA_QK), fwd),
            pl.BlockSpec((1, tb, 2 * GLA_QK), bwd),
            pl.BlockSpec((1, tb, GLA_V), fwd),
            pl.BlockSpec((1, tb, GLA_V), bwd),
            pl.BlockSpec((1, nck, GLA_NDEC, 2 * GLA_QK), lambda b, i: (b, i, 0, 0)),
            pl.BlockSpec((1, nck, GLA_NDEC, 2 * GLA_QK), lambda b, i: (b, nb - 1 - i, 0, 0)),
        ],
        out_specs=[
            pl.BlockSpec((1, tb, GLA_V), fwd),
            pl.BlockSpec((1, tb, GLA_V), bwd),
        ],
        out_shape=[
            jax.ShapeDtypeStruct((bsz, seq, GLA_V), BF16),
            jax.ShapeDtypeStruct((bsz, seq, GLA_V), BF16),
        ],
        scratch_shapes=[
            pltpu.VMEM((2 * GLA_HEADS, GLA_DK, GLA_DV), F32),
            pltpu.VMEM((2 * nck * GLA_HEADS, GLA_C, GLA_C), BF16),
            pltpu.VMEM((2 * nck * GLA_HEADS, GLA_DK, GLA_DV), F32),
        ],
        compiler_params=_params(("parallel", "arbitrary")),
        name="gla_rec",
    )(qkf, qkb, v, v, dl, dl)


def _gla_mixer(x, g_pre, w):
    bsz, seq, _ = x.shape
    x2 = x.reshape(bsz * seq, D_MODEL)
    qkf, qkb, v, gate, dl = _gla_proj(x2, g_pre, w["wm"], w["wgd"], w["wgate"], w["bgate"])
    shp = lambda a: a.reshape(bsz, seq, a.shape[-1])
    o_f, o_b = _gla_rec(shp(qkf), shp(qkb), shp(v), dl.reshape(bsz, seq // GLA_C, GLA_NDEC, 2 * GLA_QK), bsz, seq)
    return o_f.reshape(bsz * seq, GLA_V), o_b.reshape(bsz * seq, GLA_V), gate


def _rope(z, cos, sin_a, sin_b):
    return (z * cos + pltpu.roll(z, LANES - ROPE_HALF, axis=1) * sin_a
            + pltpu.roll(z, ROPE_HALF, axis=1) * sin_b)


def _swa_proj_body(x_ref, g_ref, w_ref, cos_ref, sa_ref, sb_ref, q_ref, kbd_ref, vbd_ref):
    sub = SWA_PROJ_SUB
    nsub = x_ref.shape[1] // sub
    ncol = (SWA_Q + 2 * SWA_KV) // SWA_PROJ_COLS
    per = SWA_PROJ_COLS // LANES
    low = lax.broadcasted_iota(jnp.int32, (sub, LANES), 1) < SWA_HD

    def project_pieces(s, out):
        rows = slice(s * sub, (s + 1) * sub)

        def first():
            out["xn"] = _rms(x_ref[0, rows, :], g_ref[...]).astype(BF16)

        def chunk(j):
            def run():
                cols = slice(j * SWA_PROJ_COLS, (j + 1) * SWA_PROJ_COLS)
                out["h"][j] = jnp.dot(out["xn"], w_ref[:, cols], preferred_element_type=F32)
            return run

        out["h"] = [None] * ncol
        return [first] + [chunk(j) for j in range(ncol)]

    def finish_pieces(s, src):
        rows = slice(s * sub, (s + 1) * sub)
        hblk = lambda b: src["h"][b // per][:, (b % per) * LANES:(b % per + 1) * LANES]
        rope = lambda z: _rope(z, cos_ref[rows, :], sa_ref[rows, :], sb_ref[rows, :])

        def q_piece(j):
            def run():
                q_ref[0, rows, j * LANES:(j + 1) * LANES] = (rope(hblk(j)) * SWA_QSCALE).astype(BF16)
            return run

        def spread(z, out_ref, j):
            zr = pltpu.roll(z, SWA_HD, axis=1)
            parts = (jnp.where(low, z, 0.0), jnp.where(low, 0.0, zr), jnp.where(low, zr, 0.0),
                     jnp.where(low, 0.0, z))
            for i, part in enumerate(parts):
                out_ref[0, rows, (4 * j + i) * LANES:(4 * j + i + 1) * LANES] = part.astype(BF16)

        def k_piece(j):
            return lambda: spread(rope(hblk(SWA_Q // LANES + j)), kbd_ref, j)

        def v_piece(j):
            return lambda: spread(hblk((SWA_Q + SWA_KV) // LANES + j), vbd_ref, j)

        return ([q_piece(j) for j in range(SWA_Q // LANES)] + [k_piece(j) for j in range(SWA_KV // LANES)]
                + [v_piece(j) for j in range(SWA_KV // LANES)])

    cur = {}
    _run_interleaved(project_pieces(0, cur), [])
    for s in range(nsub):
        nxt = {}
        _run_interleaved(project_pieces(s + 1, nxt) if s + 1 < nsub else [], finish_pieces(s, cur))
        cur = nxt


def _swa_proj(x, g_pre, w_in, cos, sin_a, sin_b):
    bsz, seq, _ = x.shape
    tm = SWA_PROJ_TILE
    tok = lambda w: pl.BlockSpec((1, tm, w), lambda b, i: (b, i, 0))
    tab = lambda: pl.BlockSpec((tm, LANES), lambda b, i: (i, 0))
    return pl.pallas_call(
        _swa_proj_body,
        grid=(bsz, seq // tm),
        in_specs=[tok(D_MODEL), _const_spec((1, D_MODEL)), _const_spec((D_MODEL, SWA_Q + 2 * SWA_KV)),
                  tab(), tab(), tab()],
        out_specs=[tok(SWA_Q), tok(SWA_BD), tok(SWA_BD)],
        out_shape=[
            jax.ShapeDtypeStruct((bsz, seq, SWA_Q), BF16),
            jax.ShapeDtypeStruct((bsz, seq, SWA_BD), BF16),
            jax.ShapeDtypeStruct((bsz, seq, SWA_BD), BF16),
        ],
        compiler_params=_params(("parallel", "parallel")),
        name="swa_proj",
    )(x, g_pre, w_in, cos, sin_a, sin_b)


def _swa_attn_body(nq, sink_ref, q_ref, kp_ref, km_ref, kn_ref, vp_ref, vm_ref, vn_ref,
                   x_ref, gpost_ref, wout_ref, y_ref, s_ref, p_ref, o_ref, inv_ref, bias_ref):
    qb = SWA_QB
    n = pl.program_id(1)
    blk = SWA_BLOCK
    row = lax.broadcasted_iota(jnp.int32, (blk, 2 * LANES), 0)
    col = lax.broadcasted_iota(jnp.int32, (blk, 2 * LANES), 1) & (LANES - 1)
    bias_ref[0] = jnp.where(col >= row, 0.0, NEG_BIG).astype(BF16)
    bias_ref[1] = jnp.where(col <= row, 0.0, NEG_BIG).astype(BF16)
    bias_ref[2] = jnp.full((blk, 2 * LANES), NEG_BIG, BF16)
    low = lax.broadcasted_iota(jnp.int32, (SWA_ROWS, LANES), 1) < SWA_HD
    head_a = lax.broadcasted_iota(jnp.int32, (SWA_ROWS, 2 * LANES), 1) < LANES

    def key_block(prev_ref, main_ref, next_ref, j, lanes):
        if j == 0:
            return prev_ref[0, :, lanes]
        if j == qb + 1:
            return next_ref[0, :, lanes]
        return main_ref[0, (j - 1) * blk:j * blk, lanes]

    def head_lanes(hk):
        bd = slice(2 * hk * LANES, (2 * hk + 2) * LANES)
        pair_lanes = [slice((2 * hk + e) * LANES, (2 * hk + e + 1) * LANES) for e in range(2)]
        return bd, pair_lanes

    def score_pieces(hk):
        bd, pair_lanes = head_lanes(hk)
        buf = hk % 2

        def piece(j):
            def run():
                i0, i1 = max(j - 2, 0), min(j, qb - 1)
                kj = key_block(kp_ref, km_ref, kn_ref, j, bd)
                rhs = jnp.concatenate([kj[:, :LANES], kj[:, LANES:]], axis=0)
                lhs = jnp.concatenate([q_ref[0, i0 * blk:(i1 + 1) * blk, pl_] for pl_ in pair_lanes], axis=0)
                s = lax.dot_general(lhs, rhs, (((1,), (1,)), ((), ())), preferred_element_type=F32)
                nrow = (i1 - i0 + 1) * blk
                for e in range(2):
                    for i in range(i0, i1 + 1):
                        r0 = e * nrow + (i - i0) * blk
                        s_ref[buf, e, i, j - i] = s[r0:r0 + blk, :].astype(BF16)
            return run

        return [piece(j) for j in range(qb + 2)]

    def softmax_pieces(hk):
        _, pair_lanes = head_lanes(hk)
        buf = hk % 2

        def piece(e, i):
            def run():
                sink_a = sink_ref[SWA_GROUP * hk + 2 * e] * LOG2E
                sink_b = sink_ref[SWA_GROUP * hk + 2 * e + 1] * LOG2E
                g = n * qb + i
                tbl_prev = 0 if i > 0 else jnp.where(g > 0, 0, 2)
                tbl_next = 1 if i < qb - 1 else jnp.where(g < nq - 1, 1, 2)
                for r0 in range(0, blk, SWA_ROWS):
                    rr = slice(r0, r0 + SWA_ROWS)
                    sp = s_ref[buf, e, i, 0, rr, :] + bias_ref[tbl_prev, rr, :]
                    sc = s_ref[buf, e, i, 1, rr, :]
                    sn = s_ref[buf, e, i, 2, rr, :] + bias_ref[tbl_next, rr, :]
                    mx = jnp.maximum(jnp.maximum(sp, sc), sn).astype(F32)
                    m_a = jnp.maximum(jnp.max(mx[:, :LANES], axis=-1, keepdims=True), sink_a)
                    m_b = jnp.maximum(jnp.max(mx[:, LANES:], axis=-1, keepdims=True), sink_b)
                    m = jnp.where(head_a, m_a, m_b).astype(BF16)
                    probs = [jnp.exp2(sw - m) for sw in (sp, sc, sn)]
                    tot = (probs[0] + probs[1] + probs[2]).astype(F32)
                    m32 = m.astype(F32)
                    inv_a = 1.0 / (jnp.sum(tot[:, :LANES], axis=-1, keepdims=True)
                                   + jnp.exp2(sink_a - m32[:, :1]))
                    inv_b = 1.0 / (jnp.sum(tot[:, LANES:], axis=-1, keepdims=True)
                                   + jnp.exp2(sink_b - m32[:, LANES:LANES + 1]))
                    for which, pw in enumerate(probs):
                        slot = 2 - which
                        p_ref[buf, e, i + which, slot * blk + r0:slot * blk + r0 + SWA_ROWS, :] = pw
                    inv_ref[i * blk + r0:i * blk + r0 + SWA_ROWS, pair_lanes[e]] = jnp.where(low, inv_a, inv_b)
            return run

        return [piece(e, i) for e in range(2) for i in range(qb)]

    def value_pieces(hk):
        bd, pair_lanes = head_lanes(hk)
        buf = hk % 2

        def piece(j):
            def run():
                i0, i1 = max(j - 2, 0), min(j, qb - 1)
                vj = key_block(vp_ref, vm_ref, vn_ref, j, bd)
                rhs = jnp.concatenate([vj[:, :LANES], vj[:, LANES:]], axis=0)
                r0, r1 = (i0 - (j - 2)) * blk, (i1 - (j - 2) + 1) * blk
                lhs = jnp.concatenate([p_ref[buf, e, j, r0:r1, :] for e in range(2)], axis=0)
                o = jnp.dot(lhs, rhs, preferred_element_type=F32)
                nrow = r1 - r0
                for e in range(2):
                    for i in range(i0, i1 + 1):
                        part = o[e * nrow + (i - i0) * blk:e * nrow + (i - i0 + 1) * blk, :]
                        rows = slice(i * blk, (i + 1) * blk)
                        if j == i:
                            o_ref[rows, pair_lanes[e]] = part
                        else:
                            o_ref[rows, pair_lanes[e]] += part
            return run

        return [piece(j) for j in range(qb + 2)]

    for t in range(SWA_KV_HEADS + 2):
        mm = []
        if t < SWA_KV_HEADS:
            mm += score_pieces(t)
        if 0 <= t - 2 < SWA_KV_HEADS:
            mm += value_pieces(t - 2)
        ew = softmax_pieces(t - 1) if 0 <= t - 1 < SWA_KV_HEADS else []
        _run_interleaved(mm, ew)
    o_all = (o_ref[...] * inv_ref[...]).astype(BF16)
    y = jnp.dot(o_all, wout_ref[...], preferred_element_type=F32)
    y_ref[0] = x_ref[0] + _rms(y, gpost_ref[...])


def _swa_attn(q, k, v, x, sinks, g_post, w_out):
    bsz, seq, _ = x.shape
    nq = seq // SWA_BLOCK
    qb = SWA_QB
    tq = qb * SWA_BLOCK
    cur = lambda b, n: (b, n, 0)
    prev = lambda b, n: (b, jnp.maximum(n * qb - 1, 0), 0)
    nxt = lambda b, n: (b, jnp.minimum((n + 1) * qb, nq - 1), 0)
    halo = lambda im: pl.BlockSpec((1, SWA_BLOCK, SWA_BD), im)
    main = lambda w: pl.BlockSpec((1, tq, w), cur)
    return pl.pallas_call(
        functools.partial(_swa_attn_body, nq),
        grid=(bsz, nq // qb),
        in_specs=[
            pl.BlockSpec(memory_space=pltpu.SMEM),
            main(SWA_Q),
            halo(prev), main(SWA_BD), halo(nxt), halo(prev), main(SWA_BD), halo(nxt),
            main(D_MODEL),
            _const_spec((1, D_MODEL)),
            _const_spec((SWA_Q, D_MODEL)),
        ],
        out_specs=main(D_MODEL),
        out_shape=jax.ShapeDtypeStruct((bsz, seq, D_MODEL), F32),
        scratch_shapes=[
            pltpu.VMEM((2, 2, qb, 3, SWA_BLOCK, 2 * LANES), BF16),
            pltpu.VMEM((2, 2, qb + 2, 3 * SWA_BLOCK, 2 * LANES), BF16),
            pltpu.VMEM((tq, SWA_Q), F32),
            pltpu.VMEM((tq, SWA_Q), F32),
            pltpu.VMEM((3, SWA_BLOCK, 2 * LANES), BF16),
        ],
        compiler_params=_params(("parallel", "parallel")),
        name="swa_attn",
    )(sinks, q, k, k, k, v, v, v, x, g_post, w_out)


def _rope_tables(seq):
    inv_freq = ROPE_THETA ** (-(jnp.arange(ROPE_HALF, dtype=F32) * 2.0 / ROPE_DIM))
    ang = jnp.arange(seq, dtype=F32)[:, None] * inv_freq[None, :]
    cos8, sin8 = jnp.cos(ang), jnp.sin(ang)
    ones = jnp.ones((seq, SWA_HD - ROPE_DIM), F32)
    zeros8 = jnp.zeros((seq, ROPE_HALF), F32)
    zeros = jnp.zeros((seq, SWA_HD - ROPE_DIM), F32)
    cos = jnp.concatenate([cos8, cos8, ones], axis=1)
    sin_a = jnp.concatenate([-sin8, zeros8, zeros], axis=1)
    sin_b = jnp.concatenate([zeros8, sin8, zeros], axis=1)
    rep = LANES // SWA_HD
    return tuple(jnp.tile(t, (1, rep)) for t in (cos, sin_a, sin_b))


def _swa_layer(x, g_pre, g_post, w):
    seq = x.shape[1]
    cos, sin_a, sin_b = _rope_tables(seq)
    q, k, v = _swa_proj(x, g_pre, w["win"], cos, sin_a, sin_b)
    return _swa_attn(q, k, v, x, w["sinks"], g_post, w["wout"])


def _ffn_layer(x, g2, w1r, w2, layer, which, gla=None):
    bsz, seq, _ = x.shape
    return _ffn(x.reshape(bsz * seq, D_MODEL), g2, w1r, w2, layer, which, gla).reshape(bsz, seq, D_MODEL)


def _prep_weights(ffn_w1, ffn_w2, gla_w_in, gla_w_gate_f, gla_b_gate_f, gla_w_gate_b, gla_b_gate_b,
                  gla_onorm, gla_w_out, swa_w_in, swa_sinks, swa_w_out):
    w1r = ffn_w1.astype(BF16)
    w2 = ffn_w2.astype(BF16)
    gla = []
    for j in range(gla_w_in.shape[0]):
        n_main = 2 * GLA_QK + 2 * GLA_V
        wgd = jnp.zeros((D_MODEL, LANES), F32).at[:, :2 * GLA_GATE_RANK].set(gla_w_in[j][:, n_main:])
        wgate = jnp.zeros((LANES, 2 * GLA_QK), F32)
        wgate = wgate.at[:GLA_GATE_RANK, :GLA_QK].set(gla_w_gate_f[j])
        wgate = wgate.at[GLA_GATE_RANK:2 * GLA_GATE_RANK, GLA_QK:].set(gla_w_gate_b[j])
        gla.append(dict(
            wm=gla_w_in[j][:, :n_main].astype(BF16),
            wgd=wgd.astype(BF16),
            wgate=wgate.astype(BF16),
            bgate=jnp.concatenate([gla_b_gate_f[j], gla_b_gate_b[j]])[None, :],
            onorm=gla_onorm[j][None, :],
            wout=gla_w_out[j].astype(BF16),
        ))
    swa = []
    for j in range(swa_w_in.shape[0]):
        swa.append(dict(win=swa_w_in[j].astype(BF16), sinks=swa_sinks[j], wout=swa_w_out[j].astype(BF16)))
    return w1r, w2, gla, swa


def _trunk(x, norm_g, w1r, w2, gla, swa):
    for i in range(DEPTH):
        g = norm_g[i]
        x = _ffn_layer(x, g[0:2], w1r, w2, i, 0)
        if i % 2 == 0:
            w = gla[i // 2]
            o_f, o_b, gate = _gla_mixer(x, g[2:3], w)
            x = _ffn_layer(x, g[4:6], w1r, w2, i, 1, gla=(o_f, o_b, gate, w["onorm"], g[3:4], w["wout"]))
        else:
            x = _swa_layer(x, g[2:3], g[3:4], swa[i // 2])
            x = _ffn_layer(x, g[4:6], w1r, w2, i, 1)
    return x


def kernel(x_prompt, x_sample, norm_g, ffn_w1, ffn_w2, gla_w_in, gla_w_gate_f, gla_b_gate_f, gla_w_gate_b,
           gla_b_gate_b, gla_onorm, gla_w_out, swa_w_in, swa_sinks, swa_w_out):
    w1r, w2, gla, swa = _prep_weights(ffn_w1, ffn_w2, gla_w_in, gla_w_gate_f, gla_b_gate_f, gla_w_gate_b,
                                      gla_b_gate_b, gla_onorm, gla_w_out, swa_w_in, swa_sinks, swa_w_out)
    y_prompt = _trunk(x_prompt, norm_g, w1r, w2, gla, swa)
    y_sample = _trunk(x_sample, norm_g, w1r, w2, gla, swa)
    return (y_prompt, y_sample)
```

```python
import functools

import jax
import jax.numpy as jnp
from jax import lax
from jax.experimental import pallas as pl
from jax.experimental.pallas import tpu as pltpu

F32 = jnp.float32
BF16 = jnp.bfloat16

D_MODEL = 1024
DEPTH = 4
NORM_EPS = 1e-6

D_FF = 2816
FFN_RES = 0.5
FFN_CHUNK = 256
FFN_NCHUNK = D_FF // FFN_CHUNK
FFN_TILE = 1024
FFN_SUB = 512

GLA_HEADS = 4
GLA_DK = 128
GLA_DV = 256
GLA_QK = GLA_HEADS * GLA_DK
GLA_V = GLA_HEADS * GLA_DV
GLA_GATE_RANK = 16
GLA_TAU = 16.0
GLA_C = 128
GLA_MID = GLA_C // 2 - 1
GLA_NDEC = 3

SWA_Q_HEADS = 16
SWA_KV_HEADS = 4
SWA_GROUP = SWA_Q_HEADS // SWA_KV_HEADS
SWA_HD = 64
SWA_BLOCK = 128
SWA_Q = SWA_Q_HEADS * SWA_HD
SWA_KV = SWA_KV_HEADS * SWA_HD
ROPE_THETA = 500000.0
ROPE_DIM = SWA_HD // 4
ROPE_HALF = ROPE_DIM // 2
NEG_BIG = -1e30
LOG2E = 1.4426950408889634
SWA_QSCALE = SWA_HD ** -0.5 * LOG2E

LANES = 128
SWA_BD = 2 * LANES * SWA_KV_HEADS
SWA_PROJ_TILE = 1024
SWA_PROJ_SUB = 256
SWA_PROJ_COLS = 256
SWA_QB = 4
SWA_ROWS = 128
GLA_PROJ_TILE = 1024
GLA_PROJ_SUB = 256
GLA_PROJ_COLS = 256
GLA_REC_TILE = 1024
VMEM_LIMIT = 56 * 1024 * 1024


def _rms(x, g):
    ms = jnp.mean(x * x, axis=-1, keepdims=True)
    return x * lax.rsqrt(ms + NORM_EPS) * g


def _silu(x):
    return x * (1.0 / (1.0 + jnp.exp(-x)))


def _run_interleaved(first, second):
    order = [((i + 0.5) / len(first), 0, i) for i in range(len(first))]
    order += [((j + 0.5) / len(second), 1, j) for j in range(len(second))]
    for _, which, idx in sorted(order):
        (first, second)[which][idx]()


def _const_spec(shape):
    nd = len(shape)
    return pl.BlockSpec(shape, lambda *_: (0,) * nd, pipeline_mode=pl.Buffered(1))


def _params(sem):
    return pltpu.CompilerParams(dimension_semantics=sem, vmem_limit_bytes=VMEM_LIMIT)


def _gla_tail(of_ref, ob_ref, gate_ref, x_ref, gon_ref, gmix_ref, wmix_ref, rows):
    parts = []
    for head in range(GLA_HEADS):
        lanes = slice(head * GLA_DV, (head + 1) * GLA_DV)
        oh = of_ref[rows, lanes].astype(F32) + ob_ref[rows, lanes].astype(F32)
        ms = jnp.mean(oh * oh, axis=-1, keepdims=True)
        on = oh * lax.rsqrt(ms + NORM_EPS) * gon_ref[:, lanes]
        parts.append((on * gate_ref[rows, lanes].astype(F32)).astype(BF16))
    y = jnp.dot(jnp.concatenate(parts, axis=1), wmix_ref[...], preferred_element_type=F32)
    return x_ref[rows, :] + _rms(y, gmix_ref[...])


def _ffn_body(*refs, after_gla):
    if after_gla:
        of_ref, ob_ref, gate_ref, x_ref, gon_ref, gmix_ref, wmix_ref, g_ref, w1_ref, w2_ref, o_ref, h_ref = refs
    else:
        x_ref, g_ref, w1_ref, w2_ref, o_ref, h_ref = refs
    nsub = x_ref.shape[0] // FFN_SUB
    subs = [slice(s * FFN_SUB, (s + 1) * FFN_SUB) for s in range(nsub)]
    xns = []
    for rows in subs:
        if after_gla:
            x1 = _gla_tail(of_ref, ob_ref, gate_ref, x_ref, gon_ref, gmix_ref, wmix_ref, rows)
            o_ref[rows, :] = x1
        else:
            x1 = x_ref[rows, :]
        xns.append(_rms(x1, g_ref[0:1, :]).astype(BF16))
    ys = []
    for xn in xns:
        for c in range(FFN_NCHUNK):
            cols = slice(c * FFN_CHUNK, (c + 1) * FFN_CHUNK)
            gate = jnp.dot(xn, w1_ref[:, cols], preferred_element_type=F32)
            up = jnp.dot(xn, w1_ref[:, D_FF + c * FFN_CHUNK:D_FF + (c + 1) * FFN_CHUNK],
                         preferred_element_type=F32)
            h_ref[:, cols] = (_silu(gate) * up).astype(BF16)
        ys.append(jnp.dot(h_ref[...], w2_ref[...], preferred_element_type=F32))
    base_ref = o_ref if after_gla else x_ref
    for rows, y in zip(subs, ys):
        o_ref[rows, :] = base_ref[rows, :] + FFN_RES * _rms(y, g_ref[1:2, :])


def _ffn(x2, g2, w1r, w2, layer, which, gla=None):
    t = x2.shape[0]
    tm = FFN_TILE
    pick = lambda shape: pl.BlockSpec((None, None) + shape, lambda i: (layer, which, 0, 0),
                                      pipeline_mode=pl.Buffered(1))
    tok = lambda: pl.BlockSpec((tm, D_MODEL), lambda i: (i, 0))
    ffn_specs = [_const_spec((2, D_MODEL)), pick((D_MODEL, 2 * D_FF)), pick((D_FF, D_MODEL))]
    if gla is None:
        operands, in_specs = (x2, g2, w1r, w2), [tok()] + ffn_specs
    else:
        o_f, o_b, gate, g_onorm, g_post, w_out = gla
        operands = (o_f, o_b, gate, x2, g_onorm, g_post, w_out, g2, w1r, w2)
        in_specs = ([tok(), tok(), tok(), tok(), _const_spec((1, GLA_V)), _const_spec((1, D_MODEL)),
                     _const_spec((GLA_V, D_MODEL))] + ffn_specs)
    return pl.pallas_call(
        functools.partial(_ffn_body, after_gla=gla is not None),
        grid=(t // tm,),
        in_specs=in_specs,
        out_specs=tok(),
        out_shape=jax.ShapeDtypeStruct((t, D_MODEL), F32),
        scratch_shapes=[pltpu.VMEM((FFN_SUB, D_FF), BF16)],
        compiler_params=_params(("parallel",)),
        name="ffn_gla" if gla is not None else "ffn",
    )(*operands)


def _gla_proj_body(x_ref, g_ref, wm_ref, wgd_ref, wgate_ref, bgate_ref,
                   qkf_ref, qkb_ref, v_ref, r_ref, dl_ref):
    tm = x_ref.shape[0]
    sub = GLA_PROJ_SUB
    row = lax.broadcasted_iota(jnp.int32, (GLA_C, 2 * GLA_C), 0)
    col = lax.broadcasted_iota(jnp.int32, (GLA_C, 2 * GLA_C), 1) & (GLA_C - 1)
    tri_f = (col <= row).astype(BF16)
    tri_b = (col >= row).astype(BF16)

    ncol = (2 * GLA_QK + 2 * GLA_V) // GLA_PROJ_COLS

    def project_pieces(s, out):
        rows = slice(s * sub, (s + 1) * sub)

        def first():
            out["xn"] = _rms(x_ref[rows, :], g_ref[...]).astype(BF16)
            gd = jnp.dot(out["xn"], wgd_ref[...], preferred_element_type=F32).astype(BF16)
            out["z"] = jnp.dot(gd, wgate_ref[...], preferred_element_type=F32) + bgate_ref[...]

        def chunk(j):
            def run():
                cols = slice(j * GLA_PROJ_COLS, (j + 1) * GLA_PROJ_COLS)
                out["h"][j] = jnp.dot(out["xn"], wm_ref[:, cols], preferred_element_type=F32)
            return run

        out["h"] = [None] * ncol
        return [first] + [chunk(j) for j in range(ncol)]

    def finish_pieces(s, src):
        rows = slice(s * sub, (s + 1) * sub)
        hcols = lambda lo, hi: jnp.concatenate(src["h"][lo // GLA_PROJ_COLS:hi // GLA_PROJ_COLS], axis=1)

        def cast_v():
            v_ref[rows, :] = hcols(2 * GLA_QK, 2 * GLA_QK + GLA_V).astype(BF16)

        def cast_r():
            r_ref[rows, :] = _silu(hcols(2 * GLA_QK + GLA_V, 2 * GLA_QK + 2 * GLA_V)).astype(BF16)

        def unit(cc, d):
            def run():
                tri, out_ref, last, mid = ((tri_f, qkf_ref, GLA_C - 1, GLA_MID),
                                           (tri_b, qkb_ref, 0, GLA_MID + 1))[d]
                c = s * (sub // GLA_C) + cc
                crow = slice(cc * GLA_C, (cc + 1) * GLA_C)
                orow = slice(c * GLA_C, (c + 1) * GLA_C)
                q = hcols(0, GLA_QK)[crow, :] * (GLA_DK ** -0.5)
                k = hcols(GLA_QK, 2 * GLA_QK)[crow, :]
                z = src["z"][crow, d * GLA_QK:(d + 1) * GLA_QK]
                g = (jnp.minimum(z, 0.0) - jnp.log(1.0 + jnp.exp(-jnp.abs(z)))) * (LOG2E / GLA_TAU)
                g_hi = g.astype(BF16)
                g_lo = (g - g_hi.astype(F32)).astype(BF16)
                b = jnp.dot(tri, jnp.concatenate([g_hi, g_lo], axis=0),
                            preferred_element_type=F32)
                b_mid = b[mid:mid + 1, :]
                b_last = b[last:last + 1, :]
                out_ref[orow, 0 * GLA_QK:1 * GLA_QK] = (q * jnp.exp2(b - b_mid)).astype(BF16)
                out_ref[orow, 1 * GLA_QK:2 * GLA_QK] = (k * jnp.exp2(b_mid - b)).astype(BF16)
                dcols = slice(d * GLA_QK, (d + 1) * GLA_QK)
                dl_ref[c, 0:1, dcols] = jnp.exp2(b_last)
                dl_ref[c, 1:2, dcols] = jnp.exp2(b_mid)
                dl_ref[c, 2:3, dcols] = jnp.exp2(b_last - b_mid)
            return run

        units = [unit(cc, d) for cc in range(sub // GLA_C) for d in range(2)]
        return units[:2] + [cast_v] + units[2:] + [cast_r]

    cur = {}
    _run_interleaved(project_pieces(0, cur), [])
    for s in range(tm // sub):
        nxt = {}
        _run_interleaved(project_pieces(s + 1, nxt) if s + 1 < tm // sub else [], finish_pieces(s, cur))
        cur = nxt


def _gla_proj(x2, g_pre, wm, wgd, wgate, bgate):
    t = x2.shape[0]
    tm = GLA_PROJ_TILE
    nc = tm // GLA_C
    tok = lambda w: pl.BlockSpec((tm, w), lambda i: (i, 0))
    return pl.pallas_call(
        _gla_proj_body,
        grid=(t // tm,),
        in_specs=[
            tok(D_MODEL),
            _const_spec((1, D_MODEL)),
            _const_spec((D_MODEL, 2 * GLA_QK + 2 * GLA_V)),
            _const_spec((D_MODEL, LANES)),
            _const_spec((LANES, 2 * GLA_QK)),
            _const_spec((1, 2 * GLA_QK)),
        ],
        out_specs=[
            tok(2 * GLA_QK), tok(2 * GLA_QK), tok(GLA_V), tok(GLA_V),
            pl.BlockSpec((nc, GLA_NDEC, 2 * GLA_QK), lambda i: (i, 0, 0)),
        ],
        out_shape=[
            jax.ShapeDtypeStruct((t, 2 * GLA_QK), BF16),
            jax.ShapeDtypeStruct((t, 2 * GLA_QK), BF16),
            jax.ShapeDtypeStruct((t, GLA_V), BF16),
            jax.ShapeDtypeStruct((t, GLA_V), BF16),
            jax.ShapeDtypeStruct((t // GLA_C, GLA_NDEC, 2 * GLA_QK), F32),
        ],
        compiler_params=_params(("parallel",)),
        name="gla_proj",
    )(x2, g_pre, wm, wgd, wgate, bgate)


def _gla_rec_body(qkf_ref, qkb_ref, vf_ref, vb_ref, dlf_ref, dlb_ref, of_ref, ob_ref, s_ref, a_ref, kv_ref):
    tb = qkf_ref.shape[1]
    nck = tb // GLA_C

    @pl.when(pl.program_id(1) == 0)
    def _():
        s_ref[...] = jnp.zeros_like(s_ref)

    row = lax.broadcasted_iota(jnp.int32, (GLA_C, GLA_C), 0)
    col = lax.broadcasted_iota(jnp.int32, (GLA_C, GLA_C), 1)
    dirs = ((qkf_ref, vf_ref, dlf_ref, of_ref, col <= row), (qkb_ref, vb_ref, dlb_ref, ob_ref, col >= row))
    lanes = lambda part, head: slice(part * GLA_QK + head * GLA_DK, part * GLA_QK + (head + 1) * GLA_DK)
    vlanes = lambda head: slice(head * GLA_DV, (head + 1) * GLA_DV)
    unit = lambda d, c, head: (d * nck + c) * GLA_HEADS + head

    for d, (qk_ref, v_ref, dl_ref, _, mask) in enumerate(dirs):
        for c in range(nck):
            rows = slice(c * GLA_C, (c + 1) * GLA_C)
            for head in range(GLA_HEADS):
                k_mid = qk_ref[0, rows, lanes(1, head)]
                scores = lax.dot_general(qk_ref[0, rows, lanes(0, head)], k_mid,
                                         (((1,), (1,)), ((), ())), preferred_element_type=F32)
                a_ref[unit(d, c, head)] = jnp.where(mask, scores, 0.0).astype(BF16)
                k_end = (k_mid.astype(F32) * dl_ref[0, c, 2:3, lanes(d, head)]).astype(BF16)
                kv_ref[unit(d, c, head)] = lax.dot_general(
                    k_end, v_ref[0, rows, vlanes(head)],
                    (((0,), (0,)), ((), ())), preferred_element_type=F32)

    for step in range(nck):
        for d, (qk_ref, v_ref, dl_ref, o_ref, _) in enumerate(dirs):
            c = step if d == 0 else nck - 1 - step
            rows = slice(c * GLA_C, (c + 1) * GLA_C)
            for head in range(GLA_HEADS):
                state = s_ref[d * GLA_HEADS + head]
                q_dec = (qk_ref[0, rows, lanes(0, head)].astype(F32)
                         * dl_ref[0, c, 1:2, lanes(d, head)]).astype(BF16)
                lhs = jnp.concatenate([a_ref[unit(d, c, head)], q_dec], axis=1)
                rhs = jnp.concatenate([v_ref[0, rows, vlanes(head)], state.astype(BF16)], axis=0)
                o_ref[0, rows, vlanes(head)] = jnp.dot(lhs, rhs, preferred_element_type=F32).astype(BF16)
                dl_row = dl_ref[0, c, 0:1, lanes(d, head)]
                dl_col = jnp.transpose(jnp.broadcast_to(dl_row, (GLA_DK, GLA_DK)))
                s_ref[d * GLA_HEADS + head] = (jnp.concatenate([dl_col, dl_col], axis=1) * state
                                               + kv_ref[unit(d, c, head)])


def _gla_rec(qkf, qkb, v, dl, bsz, seq):
    tb = GLA_REC_TILE
    nb = seq // tb
    nck = tb // GLA_C
    fwd = lambda b, i: (b, i, 0)
    bwd = lambda b, i: (b, nb - 1 - i, 0)
    return pl.pallas_call(
        _gla_rec_body,
        grid=(bsz, nb),
        in_specs=[
            pl.BlockSpec((1, tb, 2 * GLA_QK), fwd),
            pl.BlockSpec((1, tb, 2 * GLA_QK), bwd),
            pl.BlockSpec((1, tb, GLA_V), fwd),
            pl.BlockSpec((1, tb, GLA_V), bwd),
            pl.BlockSpec((1, nck, GLA_NDEC, 2 * GLA_QK), lambda b, i: (b, i, 0, 0)),
            pl.BlockSpec((1, nck, GLA_NDEC, 2 * GLA_QK), lambda b, i: (b, nb - 1 - i, 0, 0)),
        ],
        out_specs=[
            pl.BlockSpec((1, tb, GLA_V), fwd),
            pl.BlockSpec((1, tb, GLA_V), bwd),
        ],
        out_shape=[
            jax.ShapeDtypeStruct((bsz, seq, GLA_V), BF16),
            jax.ShapeDtypeStruct((bsz, seq, GLA_V), BF16),
        ],
        scratch_shapes=[
            pltpu.VMEM((2 * GLA_HEADS, GLA_DK, GLA_DV), F32),
            pltpu.VMEM((2 * nck * GLA_HEADS, GLA_C, GLA_C), BF16),
            pltpu.VMEM((2 * nck * GLA_HEADS, GLA_DK, GLA_DV), F32),
        ],
        compiler_params=_params(("parallel", "arbitrary")),
        name="gla_rec",
    )(qkf, qkb, v, v, dl, dl)


def _gla_mixer(x, g_pre, w):
    bsz, seq, _ = x.shape
    x2 = x.reshape(bsz * seq, D_MODEL)
    qkf, qkb, v, gate, dl = _gla_proj(x2, g_pre, w["wm"], w["wgd"], w["wgate"], w["bgate"])
    shp = lambda a: a.reshape(bsz, seq, a.shape[-1])
    o_f, o_b = _gla_rec(shp(qkf), shp(qkb), shp(v), dl.reshape(bsz, seq // GLA_C, GLA_NDEC, 2 * GLA_QK), bsz, seq)
    return o_f.reshape(bsz * seq, GLA_V), o_b.reshape(bsz * seq, GLA_V), gate


def _rope(z, cos, sin_a, sin_b):
    return (z * cos + pltpu.roll(z, LANES - ROPE_HALF, axis=1) * sin_a
            + pltpu.roll(z, ROPE_HALF, axis=1) * sin_b)


def _swa_proj_body(x_ref, g_ref, w_ref, cos_ref, sa_ref, sb_ref, q_ref, kbd_ref, vbd_ref):
    sub = SWA_PROJ_SUB
    nsub = x_ref.shape[1] // sub
    ncol = (SWA_Q + 2 * SWA_KV) // SWA_PROJ_COLS
    per = SWA_PROJ_COLS // LANES
    low = lax.broadcasted_iota(jnp.int32, (sub, LANES), 1) < SWA_HD

    def project_pieces(s, out):
        rows = slice(s * sub, (s + 1) * sub)

        def first():
            out["xn"] = _rms(x_ref[0, rows, :], g_ref[...]).astype(BF16)

        def chunk(j):
            def run():
                cols = slice(j * SWA_PROJ_COLS, (j + 1) * SWA_PROJ_COLS)
                out["h"][j] = jnp.dot(out["xn"], w_ref[:, cols], preferred_element_type=F32)
            return run

        out["h"] = [None] * ncol
        return [first] + [chunk(j) for j in range(ncol)]

    def finish_pieces(s, src):
        rows = slice(s * sub, (s + 1) * sub)
        hblk = lambda b: src["h"][b // per][:, (b % per) * LANES:(b % per + 1) * LANES]
        rope = lambda z: _rope(z, cos_ref[rows, :], sa_ref[rows, :], sb_ref[rows, :])

        def q_piece(j):
            def run():
                q_ref[0, rows, j * LANES:(j + 1) * LANES] = (rope(hblk(j)) * SWA_QSCALE).astype(BF16)
            return run

        def spread(z, out_ref, j):
            zr = pltpu.roll(z, SWA_HD, axis=1)
            parts = (jnp.where(low, z, 0.0), jnp.where(low, 0.0, zr), jnp.where(low, zr, 0.0),
                     jnp.where(low, 0.0, z))
            for i, part in enumerate(parts):
                out_ref[0, rows, (4 * j + i) * LANES:(4 * j + i + 1) * LANES] = part.astype(BF16)

        def k_piece(j):
            return lambda: spread(rope(hblk(SWA_Q // LANES + j)), kbd_ref, j)

        def v_piece(j):
            return lambda: spread(hblk((SWA_Q + SWA_KV) // LANES + j), vbd_ref, j)

        return ([q_piece(j) for j in range(SWA_Q // LANES)] + [k_piece(j) for j in range(SWA_KV // LANES)]
                + [v_piece(j) for j in range(SWA_KV // LANES)])

    cur = {}
    _run_interleaved(project_pieces(0, cur), [])
    for s in range(nsub):
        nxt = {}
        _run_interleaved(project_pieces(s + 1, nxt) if s + 1 < nsub else [], finish_pieces(s, cur))
        cur = nxt


def _swa_proj(x, g_pre, w_in, cos, sin_a, sin_b):
    bsz, seq, _ = x.shape
    tm = SWA_PROJ_TILE
    tok = lambda w: pl.BlockSpec((1, tm, w), lambda b, i: (b, i, 0))
    tab = lambda: pl.BlockSpec((tm, LANES), lambda b, i: (i, 0))
    return pl.pallas_call(
        _swa_proj_body,
        grid=(bsz, seq // tm),
        in_specs=[tok(D_MODEL), _const_spec((1, D_MODEL)), _const_spec((D_MODEL, SWA_Q + 2 * SWA_KV)),
                  tab(), tab(), tab()],
        out_specs=[tok(SWA_Q), tok(SWA_BD), tok(SWA_BD)],
        out_shape=[
            jax.ShapeDtypeStruct((bsz, seq, SWA_Q), BF16),
            jax.ShapeDtypeStruct((bsz, seq, SWA_BD), BF16),
            jax.ShapeDtypeStruct((bsz, seq, SWA_BD), BF16),
        ],
        compiler_params=_params(("parallel", "parallel")),
        name="swa_proj",
    )(x, g_pre, w_in, cos, sin_a, sin_b)


def _swa_attn_body(nq, sink_ref, q_ref, kp_ref, km_ref, kn_ref, vp_ref, vm_ref, vn_ref,
                   x_ref, gpost_ref, wout_ref, y_ref, s_ref, p_ref, o_ref, inv_ref, bias_ref):
    qb = SWA_QB
    n = pl.program_id(1)
    blk = SWA_BLOCK
    row = lax.broadcasted_iota(jnp.int32, (blk, 2 * LANES), 0)
    col = lax.broadcasted_iota(jnp.int32, (blk, 2 * LANES), 1) & (LANES - 1)
    bias_ref[0] = jnp.where(col >= row, 0.0, NEG_BIG).astype(BF16)
    bias_ref[1] = jnp.where(col <= row, 0.0, NEG_BIG).astype(BF16)
    bias_ref[2] = jnp.full((blk, 2 * LANES), NEG_BIG, BF16)
    low = lax.broadcasted_iota(jnp.int32, (SWA_ROWS, LANES), 1) < SWA_HD
    head_a = lax.broadcasted_iota(jnp.int32, (SWA_ROWS, 2 * LANES), 1) < LANES

    def key_block(prev_ref, main_ref, next_ref, j, lanes):
        if j == 0:
            return prev_ref[0, :, lanes]
        if j == qb + 1:
            return next_ref[0, :, lanes]
        return main_ref[0, (j - 1) * blk:j * blk, lanes]

    def head_lanes(hk):
        bd = slice(2 * hk * LANES, (2 * hk + 2) * LANES)
        pair_lanes = [slice((2 * hk + e) * LANES, (2 * hk + e + 1) * LANES) for e in range(2)]
        return bd, pair_lanes

    def score_pieces(hk):
        bd, pair_lanes = head_lanes(hk)
        buf = hk % 2

        def piece(j):
            def run():
                i0, i1 = max(j - 2, 0), min(j, qb - 1)
                kj = key_block(kp_ref, km_ref, kn_ref, j, bd)
                rhs = jnp.concatenate([kj[:, :LANES], kj[:, LANES:]], axis=0)
                lhs = jnp.concatenate([q_ref[0, i0 * blk:(i1 + 1) * blk, pl_] for pl_ in pair_lanes], axis=0)
                s = lax.dot_general(lhs, rhs, (((1,), (1,)), ((), ())), preferred_element_type=F32)
                nrow = (i1 - i0 + 1) * blk
                for e in range(2):
                    for i in range(i0, i1 + 1):
                        r0 = e * nrow + (i - i0) * blk
                        s_ref[buf, e, i, j - i] = s[r0:r0 + blk, :].astype(BF16)
            return run

        return [piece(j) for j in range(qb + 2)]

    def softmax_pieces(hk):
        _, pair_lanes = head_lanes(hk)
        buf = hk % 2

        def piece(e, i):
            def run():
                sink_a = sink_ref[SWA_GROUP * hk + 2 * e] * LOG2E
                sink_b = sink_ref[SWA_GROUP * hk + 2 * e + 1] * LOG2E
                g = n * qb + i
                tbl_prev = 0 if i > 0 else jnp.where(g > 0, 0, 2)
                tbl_next = 1 if i < qb - 1 else jnp.where(g < nq - 1, 1, 2)
                for r0 in range(0, blk, SWA_ROWS):
                    rr = slice(r0, r0 + SWA_ROWS)
                    sp = s_ref[buf, e, i, 0, rr, :] + bias_ref[tbl_prev, rr, :]
                    sc = s_ref[buf, e, i, 1, rr, :]
                    sn = s_ref[buf, e, i, 2, rr, :] + bias_ref[tbl_next, rr, :]
                    mx = jnp.maximum(jnp.maximum(sp, sc), sn).astype(F32)
                    m_a = jnp.maximum(jnp.max(mx[:, :LANES], axis=-1, keepdims=True), sink_a)
                    m_b = jnp.maximum(jnp.max(mx[:, LANES:], axis=-1, keepdims=True), sink_b)
                    m = jnp.where(head_a, m_a, m_b).astype(BF16)
                    probs = [jnp.exp2(sw - m) for sw in (sp, sc, sn)]
                    tot = (probs[0] + probs[1] + probs[2]).astype(F32)
                    m32 = m.astype(F32)
                    inv_a = 1.0 / (jnp.sum(tot[:, :LANES], axis=-1, keepdims=True)
                                   + jnp.exp2(sink_a - m32[:, :1]))
                    inv_b = 1.0 / (jnp.sum(tot[:, LANES:], axis=-1, keepdims=True)
                                   + jnp.exp2(sink_b - m32[:, LANES:LANES + 1]))
                    for which, pw in enumerate(probs):
                        slot = 2 - which
                        p_ref[buf, e, i + which, slot * blk + r0:slot * blk + r0 + SWA_ROWS, :] = pw
                    inv_ref[i * blk + r0:i * blk + r0 + SWA_ROWS, pair_lanes[e]] = jnp.where(low, inv_a, inv_b)
            return run

        return [piece(e, i) for e in range(2) for i in range(qb)]

    def value_pieces(hk):
        bd, pair_lanes = head_lanes(hk)
        buf = hk % 2

        def piece(j):
            def run():
                i0, i1 = max(j - 2, 0), min(j, qb - 1)
                vj = key_block(vp_ref, vm_ref, vn_ref, j, bd)
                rhs = jnp.concatenate([vj[:, :LANES], vj[:, LANES:]], axis=0)
                r0, r1 = (i0 - (j - 2)) * blk, (i1 - (j - 2) + 1) * blk
                lhs = jnp.concatenate([p_ref[buf, e, j, r0:r1, :] for e in range(2)], axis=0)
                o = jnp.dot(lhs, rhs, preferred_element_type=F32)
                nrow = r1 - r0
                for e in range(2):
                    for i in range(i0, i1 + 1):
                        part = o[e * nrow + (i - i0) * blk:e * nrow + (i - i0 + 1) * blk, :]
                        rows = slice(i * blk, (i + 1) * blk)
                        if j == i:
                            o_ref[rows, pair_lanes[e]] = part
                        else:
                            o_ref[rows, pair_lanes[e]] += part
            return run

        return [piece(j) for j in range(qb + 2)]

    for t in range(SWA_KV_HEADS + 2):
        sc = score_pieces(t) if t < SWA_KV_HEADS else []
        va = value_pieces(t - 2) if 0 <= t - 2 < SWA_KV_HEADS else []
        mm = [p for pair in zip(va, sc) for p in pair] if sc and va else va + sc
        ew = softmax_pieces(t - 1) if 0 <= t - 1 < SWA_KV_HEADS else []
        _run_interleaved(mm, ew)
    o_all = (o_ref[...] * inv_ref[...]).astype(BF16)
    y = jnp.dot(o_all, wout_ref[...], preferred_element_type=F32)
    y_ref[0] = x_ref[0] + _rms(y, gpost_ref[...])


def _swa_attn(q, k, v, x, sinks, g_post, w_out):
    bsz, seq, _ = x.shape
    nq = seq // SWA_BLOCK
    qb = SWA_QB
    tq = qb * SWA_BLOCK
    cur = lambda b, n: (b, n, 0)
    prev = lambda b, n: (b, jnp.maximum(n * qb - 1, 0), 0)
    nxt = lambda b, n: (b, jnp.minimum((n + 1) * qb, nq - 1), 0)
    halo = lambda im: pl.BlockSpec((1, SWA_BLOCK, SWA_BD), im)
    main = lambda w: pl.BlockSpec((1, tq, w), cur)
    return pl.pallas_call(
        functools.partial(_swa_attn_body, nq),
        grid=(bsz, nq // qb),
        in_specs=[
            pl.BlockSpec(memory_space=pltpu.SMEM),
            main(SWA_Q),
            halo(prev), main(SWA_BD), halo(nxt), halo(prev), main(SWA_BD), halo(nxt),
            main(D_MODEL),
            _const_spec((1, D_MODEL)),
            _const_spec((SWA_Q, D_MODEL)),
        ],
        out_specs=main(D_MODEL),
        out_shape=jax.ShapeDtypeStruct((bsz, seq, D_MODEL), F32),
        scratch_shapes=[
            pltpu.VMEM((2, 2, qb, 3, SWA_BLOCK, 2 * LANES), BF16),
            pltpu.VMEM((2, 2, qb + 2, 3 * SWA_BLOCK, 2 * LANES), BF16),
            pltpu.VMEM((tq, SWA_Q), F32),
            pltpu.VMEM((tq, SWA_Q), F32),
            pltpu.VMEM((3, SWA_BLOCK, 2 * LANES), BF16),
        ],
        compiler_params=_params(("parallel", "parallel")),
        name="swa_attn",
    )(sinks, q, k, k, k, v, v, v, x, g_post, w_out)


def _rope_tables(seq):
    inv_freq = ROPE_THETA ** (-(jnp.arange(ROPE_HALF, dtype=F32) * 2.0 / ROPE_DIM))
    ang = jnp.arange(seq, dtype=F32)[:, None] * inv_freq[None, :]
    cos8, sin8 = jnp.cos(ang), jnp.sin(ang)
    ones = jnp.ones((seq, SWA_HD - ROPE_DIM), F32)
    zeros8 = jnp.zeros((seq, ROPE_HALF), F32)
    zeros = jnp.zeros((seq, SWA_HD - ROPE_DIM), F32)
    cos = jnp.concatenate([cos8, cos8, ones], axis=1)
    sin_a = jnp.concatenate([-sin8, zeros8, zeros], axis=1)
    sin_b = jnp.concatenate([zeros8, sin8, zeros], axis=1)
    rep = LANES // SWA_HD
    return tuple(jnp.tile(t, (1, rep)) for t in (cos, sin_a, sin_b))


def _swa_layer(x, g_pre, g_post, w):
    seq = x.shape[1]
    cos, sin_a, sin_b = _rope_tables(seq)
    q, k, v = _swa_proj(x, g_pre, w["win"], cos, sin_a, sin_b)
    return _swa_attn(q, k, v, x, w["sinks"], g_post, w["wout"])


def _ffn_layer(x, g2, w1r, w2, layer, which, gla=None):
    bsz, seq, _ = x.shape
    return _ffn(x.reshape(bsz * seq, D_MODEL), g2, w1r, w2, layer, which, gla).reshape(bsz, seq, D_MODEL)


def _prep_weights(ffn_w1, ffn_w2, gla_w_in, gla_w_gate_f, gla_b_gate_f, gla_w_gate_b, gla_b_gate_b,
                  gla_onorm, gla_w_out, swa_w_in, swa_sinks, swa_w_out):
    w1r = ffn_w1.astype(BF16)
    w2 = ffn_w2.astype(BF16)
    gla = []
    for j in range(gla_w_in.shape[0]):
        n_main = 2 * GLA_QK + 2 * GLA_V
        wgd = jnp.zeros((D_MODEL, LANES), F32).at[:, :2 * GLA_GATE_RANK].set(gla_w_in[j][:, n_main:])
        wgate = jnp.zeros((LANES, 2 * GLA_QK), F32)
        wgate = wgate.at[:GLA_GATE_RANK, :GLA_QK].set(gla_w_gate_f[j])
        wgate = wgate.at[GLA_GATE_RANK:2 * GLA_GATE_RANK, GLA_QK:].set(gla_w_gate_b[j])
        gla.append(dict(
            wm=gla_w_in[j][:, :n_main].astype(BF16),
            wgd=wgd.astype(BF16),
            wgate=wgate.astype(BF16),
            bgate=jnp.concatenate([gla_b_gate_f[j], gla_b_gate_b[j]])[None, :],
            onorm=gla_onorm[j][None, :],
            wout=gla_w_out[j].astype(BF16),
        ))
    swa = []
    for j in range(swa_w_in.shape[0]):
        swa.append(dict(win=swa_w_in[j].astype(BF16), sinks=swa_sinks[j], wout=swa_w_out[j].astype(BF16)))
    return w1r, w2, gla, swa


def _trunk(x, norm_g, w1r, w2, gla, swa):
    for i in range(DEPTH):
        g = norm_g[i]
        x = _ffn_layer(x, g[0:2], w1r, w2, i, 0)
        if i % 2 == 0:
            w = gla[i // 2]
            o_f, o_b, gate = _gla_mixer(x, g[2:3], w)
            x = _ffn_layer(x, g[4:6], w1r, w2, i, 1, gla=(o_f, o_b, gate, w["onorm"], g[3:4], w["wout"]))
        else:
            x = _swa_layer(x, g[2:3], g[3:4], swa[i // 2])
            x = _ffn_layer(x, g[4:6], w1r, w2, i, 1)
    return x


def kernel(x_prompt, x_sample, norm_g, ffn_w1, ffn_w2, gla_w_in, gla_w_gate_f, gla_b_gate_f, gla_w_gate_b,
           gla_b_gate_b, gla_onorm, gla_w_out, swa_w_in, swa_sinks, swa_w_out):
    w1r, w2, gla, swa = _prep_weights(ffn_w1, ffn_w2, gla_w_in, gla_w_gate_f, gla_b_gate_f, gla_w_gate_b,
                                      gla_b_gate_b, gla_onorm, gla_w_out, swa_w_in, swa_sinks, swa_w_out)
    y_prompt = _trunk(x_prompt, norm_g, w1r, w2, gla, swa)
    y_sample = _trunk(x_sample, norm_g, w1r, w2, gla, swa)
    return (y_prompt, y_sample)
```

```python
import functools

import jax
import jax.numpy as jnp
from jax import lax
from jax.experimental import pallas as pl
from jax.experimental.pallas import tpu as pltpu

F32 = jnp.float32
BF16 = jnp.bfloat16

D_MODEL = 1024
DEPTH = 4
NORM_EPS = 1e-6

D_FF = 2816
FFN_RES = 0.5
FFN_CHUNK = 256
FFN_NCHUNK = D_FF // FFN_CHUNK
FFN_TILE = 1024
FFN_SUB = 512

GLA_HEADS = 4
GLA_DK = 128
GLA_DV = 256
GLA_QK = GLA_HEADS * GLA_DK
GLA_V = GLA_HEADS * GLA_DV
GLA_GATE_RANK = 16
GLA_TAU = 16.0
GLA_C = 128
GLA_MID = GLA_C // 2 - 1
GLA_NDEC = 3

SWA_Q_HEADS = 16
SWA_KV_HEADS = 4
SWA_GROUP = SWA_Q_HEADS // SWA_KV_HEADS
SWA_HD = 64
SWA_BLOCK = 128
SWA_Q = SWA_Q_HEADS * SWA_HD
SWA_KV = SWA_KV_HEADS * SWA_HD
ROPE_THETA = 500000.0
ROPE_DIM = SWA_HD // 4
ROPE_HALF = ROPE_DIM // 2
NEG_BIG = -1e30
LOG2E = 1.4426950408889634
SWA_QSCALE = SWA_HD ** -0.5 * LOG2E

LANES = 128
SWA_BD = 2 * LANES * SWA_KV_HEADS
SWA_PROJ_TILE = 1024
SWA_PROJ_SUB = 256
SWA_PROJ_COLS = 256
SWA_QB = 4
SWA_ROWS = 128
GLA_PROJ_TILE = 1024
GLA_PROJ_SUB = 256
GLA_PROJ_COLS = 256
GLA_REC_TILE = 1024
VMEM_LIMIT = 56 * 1024 * 1024


def _rms(x, g):
    ms = jnp.mean(x * x, axis=-1, keepdims=True)
    return x * lax.rsqrt(ms + NORM_EPS) * g


def _silu(x):
    return x * (1.0 / (1.0 + jnp.exp(-x)))


def _run_interleaved(first, second):
    order = [((i + 0.5) / len(first), 0, i) for i in range(len(first))]
    order += [((j + 0.5) / len(second), 1, j) for j in range(len(second))]
    for _, which, idx in sorted(order):
        (first, second)[which][idx]()


def _const_spec(shape):
    nd = len(shape)
    return pl.BlockSpec(shape, lambda *_: (0,) * nd, pipeline_mode=pl.Buffered(1))


def _params(sem):
    return pltpu.CompilerParams(dimension_semantics=sem, vmem_limit_bytes=VMEM_LIMIT)


def _gla_tail(of_ref, ob_ref, gate_ref, x_ref, gon_ref, gmix_ref, wmix_ref, rows):
    parts = []
    for head in range(GLA_HEADS):
        lanes = slice(head * GLA_DV, (head + 1) * GLA_DV)
        oh = of_ref[rows, lanes].astype(F32) + ob_ref[rows, lanes].astype(F32)
        ms = jnp.mean(oh * oh, axis=-1, keepdims=True)
        on = oh * lax.rsqrt(ms + NORM_EPS) * gon_ref[:, lanes]
        parts.append((on * gate_ref[rows, lanes].astype(F32)).astype(BF16))
    y = jnp.dot(jnp.concatenate(parts, axis=1), wmix_ref[...], preferred_element_type=F32)
    return x_ref[rows, :] + _rms(y, gmix_ref[...])


def _ffn_body(*refs, after_gla):
    if after_gla:
        of_ref, ob_ref, gate_ref, x_ref, gon_ref, gmix_ref, wmix_ref, g_ref, w1_ref, w2_ref, o_ref, h_ref = refs
    else:
        x_ref, g_ref, w1_ref, w2_ref, o_ref, h_ref = refs
    nsub = x_ref.shape[0] // FFN_SUB
    subs = [slice(s * FFN_SUB, (s + 1) * FFN_SUB) for s in range(nsub)]
    xns = []
    for rows in subs:
        if after_gla:
            x1 = _gla_tail(of_ref, ob_ref, gate_ref, x_ref, gon_ref, gmix_ref, wmix_ref, rows)
            o_ref[rows, :] = x1
        else:
            x1 = x_ref[rows, :]
        xns.append(_rms(x1, g_ref[0:1, :]).astype(BF16))
    ys = []
    for xn in xns:
        for c in range(FFN_NCHUNK):
            cols = slice(c * FFN_CHUNK, (c + 1) * FFN_CHUNK)
            gate = jnp.dot(xn, w1_ref[:, cols], preferred_element_type=F32)
            up = jnp.dot(xn, w1_ref[:, D_FF + c * FFN_CHUNK:D_FF + (c + 1) * FFN_CHUNK],
                         preferred_element_type=F32)
            h_ref[:, cols] = (_silu(gate) * up).astype(BF16)
        ys.append(jnp.dot(h_ref[...], w2_ref[...], preferred_element_type=F32))
    base_ref = o_ref if after_gla else x_ref
    for rows, y in zip(subs, ys):
        o_ref[rows, :] = base_ref[rows, :] + FFN_RES * _rms(y, g_ref[1:2, :])


def _ffn(x2, g2, w1r, w2, layer, which, gla=None):
    t = x2.shape[0]
    tm = FFN_TILE
    pick = lambda shape: pl.BlockSpec((None, None) + shape, lambda i: (layer, which, 0, 0),
                                      pipeline_mode=pl.Buffered(1))
    tok = lambda: pl.BlockSpec((tm, D_MODEL), lambda i: (i, 0))
    ffn_specs = [_const_spec((2, D_MODEL)), pick((D_MODEL, 2 * D_FF)), pick((D_FF, D_MODEL))]
    if gla is None:
        operands, in_specs = (x2, g2, w1r, w2), [tok()] + ffn_specs
    else:
        o_f, o_b, gate, g_onorm, g_post, w_out = gla
        operands = (o_f, o_b, gate, x2, g_onorm, g_post, w_out, g2, w1r, w2)
        in_specs = ([tok(), tok(), tok(), tok(), _const_spec((1, GLA_V)), _const_spec((1, D_MODEL)),
                     _const_spec((GLA_V, D_MODEL))] + ffn_specs)
    return pl.pallas_call(
        functools.partial(_ffn_body, after_gla=gla is not None),
        grid=(t // tm,),
        in_specs=in_specs,
        out_specs=tok(),
        out_shape=jax.ShapeDtypeStruct((t, D_MODEL), F32),
        scratch_shapes=[pltpu.VMEM((FFN_SUB, D_FF), BF16)],
        compiler_params=_params(("parallel",)),
        name="ffn_gla" if gla is not None else "ffn",
    )(*operands)


def _gla_proj_body(x_ref, g_ref, wm_ref, wgd_ref, wgate_ref, bgate_ref,
                   qkf_ref, qkb_ref, v_ref, r_ref, dl_ref):
    tm = x_ref.shape[0]
    sub = GLA_PROJ_SUB
    row = lax.broadcasted_iota(jnp.int32, (GLA_C, 2 * GLA_C), 0)
    col = lax.broadcasted_iota(jnp.int32, (GLA_C, 2 * GLA_C), 1) & (GLA_C - 1)
    tri_f = (col <= row).astype(BF16)
    tri_b = (col >= row).astype(BF16)

    ncol = (2 * GLA_QK + 2 * GLA_V) // GLA_PROJ_COLS

    def project_pieces(s, out):
        rows = slice(s * sub, (s + 1) * sub)

        def first():
            out["xn"] = _rms(x_ref[rows, :], g_ref[...]).astype(BF16)
            gd = jnp.dot(out["xn"], wgd_ref[...], preferred_element_type=F32).astype(BF16)
            out["z"] = jnp.dot(gd, wgate_ref[...], preferred_element_type=F32) + bgate_ref[...]

        def chunk(j):
            def run():
                cols = slice(j * GLA_PROJ_COLS, (j + 1) * GLA_PROJ_COLS)
                out["h"][j] = jnp.dot(out["xn"], wm_ref[:, cols], preferred_element_type=F32)
            return run

        out["h"] = [None] * ncol
        return [first] + [chunk(j) for j in range(ncol)]

    def finish_pieces(s, src):
        rows = slice(s * sub, (s + 1) * sub)
        hcols = lambda lo, hi: jnp.concatenate(src["h"][lo // GLA_PROJ_COLS:hi // GLA_PROJ_COLS], axis=1)

        def cast_v():
            v_ref[rows, :] = hcols(2 * GLA_QK, 2 * GLA_QK + GLA_V).astype(BF16)

        def cast_r():
            r_ref[rows, :] = _silu(hcols(2 * GLA_QK + GLA_V, 2 * GLA_QK + 2 * GLA_V)).astype(BF16)

        def unit(cc, d):
            def run():
                tri, out_ref, last, mid = ((tri_f, qkf_ref, GLA_C - 1, GLA_MID),
                                           (tri_b, qkb_ref, 0, GLA_MID + 1))[d]
                c = s * (sub // GLA_C) + cc
                crow = slice(cc * GLA_C, (cc + 1) * GLA_C)
                orow = slice(c * GLA_C, (c + 1) * GLA_C)
                q = hcols(0, GLA_QK)[crow, :] * (GLA_DK ** -0.5)
                k = hcols(GLA_QK, 2 * GLA_QK)[crow, :]
                z = src["z"][crow, d * GLA_QK:(d + 1) * GLA_QK]
                g = (jnp.minimum(z, 0.0) - jnp.log(1.0 + jnp.exp(-jnp.abs(z)))) * (LOG2E / GLA_TAU)
                g_hi = g.astype(BF16)
                g_lo = (g - g_hi.astype(F32)).astype(BF16)
                b = jnp.dot(tri, jnp.concatenate([g_hi, g_lo], axis=0),
                            preferred_element_type=F32)
                b_mid = b[mid:mid + 1, :]
                b_last = b[last:last + 1, :]
                out_ref[orow, 0 * GLA_QK:1 * GLA_QK] = (q * jnp.exp2(b - b_mid)).astype(BF16)
                out_ref[orow, 1 * GLA_QK:2 * GLA_QK] = (k * jnp.exp2(b_mid - b)).astype(BF16)
                dcols = slice(d * GLA_QK, (d + 1) * GLA_QK)
                dl_ref[c, 0:1, dcols] = jnp.exp2(b_last)
                dl_ref[c, 1:2, dcols] = jnp.exp2(b_mid)
                dl_ref[c, 2:3, dcols] = jnp.exp2(b_last - b_mid)
            return run

        units = [unit(cc, d) for cc in range(sub // GLA_C) for d in range(2)]
        return units[:2] + [cast_v] + units[2:] + [cast_r]

    cur = {}
    _run_interleaved(project_pieces(0, cur), [])
    for s in range(tm // sub):
        nxt = {}
        _run_interleaved(project_pieces(s + 1, nxt) if s + 1 < tm // sub else [], finish_pieces(s, cur))
        cur = nxt


def _gla_proj(x2, g_pre, wm, wgd, wgate, bgate):
    t = x2.shape[0]
    tm = GLA_PROJ_TILE
    nc = tm // GLA_C
    tok = lambda w: pl.BlockSpec((tm, w), lambda i: (i, 0))
    return pl.pallas_call(
        _gla_proj_body,
        grid=(t // tm,),
        in_specs=[
            tok(D_MODEL),
            _const_spec((1, D_MODEL)),
            _const_spec((D_MODEL, 2 * GLA_QK + 2 * GLA_V)),
            _const_spec((D_MODEL, LANES)),
            _const_spec((LANES, 2 * GLA_QK)),
            _const_spec((1, 2 * GLA_QK)),
        ],
        out_specs=[
            tok(2 * GLA_QK), tok(2 * GLA_QK), tok(GLA_V), tok(GLA_V),
            pl.BlockSpec((nc, GLA_NDEC, 2 * GLA_QK), lambda i: (i, 0, 0)),
        ],
        out_shape=[
            jax.ShapeDtypeStruct((t, 2 * GLA_QK), BF16),
            jax.ShapeDtypeStruct((t, 2 * GLA_QK), BF16),
            jax.ShapeDtypeStruct((t, GLA_V), BF16),
            jax.ShapeDtypeStruct((t, GLA_V), BF16),
            jax.ShapeDtypeStruct((t // GLA_C, GLA_NDEC, 2 * GLA_QK), F32),
        ],
        compiler_params=_params(("parallel",)),
        name="gla_proj",
    )(x2, g_pre, wm, wgd, wgate, bgate)


def _gla_rec_body(qkf_ref, qkb_ref, vf_ref, vb_ref, dlf_ref, dlb_ref, of_ref, ob_ref, s_ref, a_ref, kv_ref):
    tb = qkf_ref.shape[1]
    nck = tb // GLA_C

    @pl.when(pl.program_id(1) == 0)
    def _():
        s_ref[...] = jnp.zeros_like(s_ref)

    row = lax.broadcasted_iota(jnp.int32, (GLA_C, GLA_C), 0)
    col = lax.broadcasted_iota(jnp.int32, (GLA_C, GLA_C), 1)
    dirs = ((qkf_ref, vf_ref, dlf_ref, of_ref, col <= row), (qkb_ref, vb_ref, dlb_ref, ob_ref, col >= row))
    lanes = lambda part, head: slice(part * GLA_QK + head * GLA_DK, part * GLA_QK + (head + 1) * GLA_DK)
    vlanes = lambda head: slice(head * GLA_DV, (head + 1) * GLA_DV)
    unit = lambda d, c, head: (d * nck + c) * GLA_HEADS + head

    for step in range(nck):
        for d, (qk_ref, v_ref, dl_ref, _, mask) in enumerate(dirs):
            c = step if d == 0 else nck - 1 - step
            rows = slice(c * GLA_C, (c + 1) * GLA_C)
            for head in range(GLA_HEADS):
                k_mid = qk_ref[0, rows, lanes(1, head)]
                scores = lax.dot_general(qk_ref[0, rows, lanes(0, head)], k_mid,
                                         (((1,), (1,)), ((), ())), preferred_element_type=F32)
                a_ref[unit(d, c, head)] = jnp.where(mask, scores, 0.0).astype(BF16)
                k_end = (k_mid.astype(F32) * dl_ref[0, c, 2:3, lanes(d, head)]).astype(BF16)
                kv_ref[unit(d, c, head)] = lax.dot_general(
                    k_end, v_ref[0, rows, vlanes(head)],
                    (((0,), (0,)), ((), ())), preferred_element_type=F32)

    for step in range(nck):
        for d, (qk_ref, v_ref, dl_ref, o_ref, _) in enumerate(dirs):
            c = step if d == 0 else nck - 1 - step
            rows = slice(c * GLA_C, (c + 1) * GLA_C)
            for head in range(GLA_HEADS):
                state = s_ref[d * GLA_HEADS + head]
                q_dec = (qk_ref[0, rows, lanes(0, head)].astype(F32)
                         * dl_ref[0, c, 1:2, lanes(d, head)]).astype(BF16)
                lhs = jnp.concatenate([a_ref[unit(d, c, head)], q_dec], axis=1)
                rhs = jnp.concatenate([v_ref[0, rows, vlanes(head)], state.astype(BF16)], axis=0)
                o_ref[0, rows, vlanes(head)] = jnp.dot(lhs, rhs, preferred_element_type=F32).astype(BF16)
                dl_row = dl_ref[0, c, 0:1, lanes(d, head)]
                dl_col = jnp.transpose(jnp.broadcast_to(dl_row, (GLA_DK, GLA_DK)))
                s_ref[d * GLA_HEADS + head] = (jnp.concatenate([dl_col, dl_col], axis=1) * state
                                               + kv_ref[unit(d, c, head)])


def _gla_rec(qkf, qkb, v, dl, bsz, seq):
    tb = GLA_REC_TILE
    nb = seq // tb
    nck = tb // GLA_C
    fwd = lambda b, i: (b, i, 0)
    bwd = lambda b, i: (b, nb - 1 - i, 0)
    return pl.pallas_call(
        _gla_rec_body,
        grid=(bsz, nb),
        in_specs=[
            pl.BlockSpec((1, tb, 2 * GLA_QK), fwd),
            pl.BlockSpec((1, tb, 2 * GLA_QK), bwd),
            pl.BlockSpec((1, tb, GLA_V), fwd),
            pl.BlockSpec((1, tb, GLA_V), bwd),
            pl.BlockSpec((1, nck, GLA_NDEC, 2 * GLA_QK), lambda b, i: (b, i, 0, 0)),
            pl.BlockSpec((1, nck, GLA_NDEC, 2 * GLA_QK), lambda b, i: (b, nb - 1 - i, 0, 0)),
        ],
        out_specs=[
            pl.BlockSpec((1, tb, GLA_V), fwd),
            pl.BlockSpec((1, tb, GLA_V), bwd),
        ],
        out_shape=[
            jax.ShapeDtypeStruct((bsz, seq, GLA_V), BF16),
            jax.ShapeDtypeStruct((bsz, seq, GLA_V), BF16),
        ],
        scratch_shapes=[
            pltpu.VMEM((2 * GLA_HEADS, GLA_DK, GLA_DV), F32),
            pltpu.VMEM((2 * nck * GLA_HEADS, GLA_C, GLA_C), BF16),
            pltpu.VMEM((2 * nck * GLA_HEADS, GLA_DK, GLA_DV), F32),
        ],
        compiler_params=_params(("parallel", "arbitrary")),
        name="gla_rec",
    )(qkf, qkb, v, v, dl, dl)


def _gla_mixer(x, g_pre, w):
    bsz, seq, _ = x.shape
    x2 = x.reshape(bsz * seq, D_MODEL)
    qkf, qkb, v, gate, dl = _gla_proj(x2, g_pre, w["wm"], w["wgd"], w["wgate"], w["bgate"])
    shp = lambda a: a.reshape(bsz, seq, a.shape[-1])
    o_f, o_b = _gla_rec(shp(qkf), shp(qkb), shp(v), dl.reshape(bsz, seq // GLA_C, GLA_NDEC, 2 * GLA_QK), bsz, seq)
    return o_f.reshape(bsz * seq, GLA_V), o_b.reshape(bsz * seq, GLA_V), gate


def _rope(z, cos, sin_a, sin_b):
    return (z * cos + pltpu.roll(z, LANES - ROPE_HALF, axis=1) * sin_a
            + pltpu.roll(z, ROPE_HALF, axis=1) * sin_b)


def _swa_proj_body(x_ref, g_ref, w_ref, cos_ref, sa_ref, sb_ref, q_ref, kbd_ref, vbd_ref):
    sub = SWA_PROJ_SUB
    nsub = x_ref.shape[1] // sub
    ncol = (SWA_Q + 2 * SWA_KV) // SWA_PROJ_COLS
    per = SWA_PROJ_COLS // LANES
    low = lax.broadcasted_iota(jnp.int32, (sub, LANES), 1) < SWA_HD

    def project_pieces(s, out):
        rows = slice(s * sub, (s + 1) * sub)

        def first():
            out["xn"] = _rms(x_ref[0, rows, :], g_ref[...]).astype(BF16)

        def chunk(j):
            def run():
                cols = slice(j * SWA_PROJ_COLS, (j + 1) * SWA_PROJ_COLS)
                out["h"][j] = jnp.dot(out["xn"], w_ref[:, cols], preferred_element_type=F32)
            return run

        out["h"] = [None] * ncol
        return [first] + [chunk(j) for j in range(ncol)]

    def finish_pieces(s, src):
        rows = slice(s * sub, (s + 1) * sub)
        hblk = lambda b: src["h"][b // per][:, (b % per) * LANES:(b % per + 1) * LANES]
        rope = lambda z: _rope(z, cos_ref[rows, :], sa_ref[rows, :], sb_ref[rows, :])

        def q_piece(j):
            def run():
                q_ref[0, rows, j * LANES:(j + 1) * LANES] = (rope(hblk(j)) * SWA_QSCALE).astype(BF16)
            return run

        def spread(z, out_ref, j):
            zr = pltpu.roll(z, SWA_HD, axis=1)
            parts = (jnp.where(low, z, 0.0), jnp.where(low, 0.0, zr), jnp.where(low, zr, 0.0),
                     jnp.where(low, 0.0, z))
            for i, part in enumerate(parts):
                out_ref[0, rows, (4 * j + i) * LANES:(4 * j + i + 1) * LANES] = part.astype(BF16)

        def k_piece(j):
            return lambda: spread(rope(hblk(SWA_Q // LANES + j)), kbd_ref, j)

        def v_piece(j):
            return lambda: spread(hblk((SWA_Q + SWA_KV) // LANES + j), vbd_ref, j)

        return ([q_piece(j) for j in range(SWA_Q // LANES)] + [k_piece(j) for j in range(SWA_KV // LANES)]
                + [v_piece(j) for j in range(SWA_KV // LANES)])

    cur = {}
    _run_interleaved(project_pieces(0, cur), [])
    for s in range(nsub):
        nxt = {}
        _run_interleaved(project_pieces(s + 1, nxt) if s + 1 < nsub else [], finish_pieces(s, cur))
        cur = nxt


def _swa_proj(x, g_pre, w_in, cos, sin_a, sin_b):
    bsz, seq, _ = x.shape
    tm = SWA_PROJ_TILE
    tok = lambda w: pl.BlockSpec((1, tm, w), lambda b, i: (b, i, 0))
    tab = lambda: pl.BlockSpec((tm, LANES), lambda b, i: (i, 0))
    return pl.pallas_call(
        _swa_proj_body,
        grid=(bsz, seq // tm),
        in_specs=[tok(D_MODEL), _const_spec((1, D_MODEL)), _const_spec((D_MODEL, SWA_Q + 2 * SWA_KV)),
                  tab(), tab(), tab()],
        out_specs=[tok(SWA_Q), tok(SWA_BD), tok(SWA_BD)],
        out_shape=[
            jax.ShapeDtypeStruct((bsz, seq, SWA_Q), BF16),
            jax.ShapeDtypeStruct((bsz, seq, SWA_BD), BF16),
            jax.ShapeDtypeStruct((bsz, seq, SWA_BD), BF16),
        ],
        compiler_params=_params(("parallel", "parallel")),
        name="swa_proj",
    )(x, g_pre, w_in, cos, sin_a, sin_b)


def _swa_attn_body(nq, sink_ref, q_ref, kp_ref, km_ref, kn_ref, vp_ref, vm_ref, vn_ref,
                   x_ref, gpost_ref, wout_ref, y_ref, s_ref, p_ref, o_ref, inv_ref, bias_ref):
    qb = SWA_QB
    n = pl.program_id(1)
    blk = SWA_BLOCK
    row = lax.broadcasted_iota(jnp.int32, (blk, 2 * LANES), 0)
    col = lax.broadcasted_iota(jnp.int32, (blk, 2 * LANES), 1) & (LANES - 1)
    bias_ref[0] = jnp.where(col >= row, 0.0, NEG_BIG).astype(BF16)
    bias_ref[1] = jnp.where(col <= row, 0.0, NEG_BIG).astype(BF16)
    bias_ref[2] = jnp.full((blk, 2 * LANES), NEG_BIG, BF16)
    low = lax.broadcasted_iota(jnp.int32, (SWA_ROWS, LANES), 1) < SWA_HD
    head_a = lax.broadcasted_iota(jnp.int32, (SWA_ROWS, 2 * LANES), 1) < LANES

    def key_block(prev_ref, main_ref, next_ref, j, lanes):
        if j == 0:
            return prev_ref[0, :, lanes]
        if j == qb + 1:
            return next_ref[0, :, lanes]
        return main_ref[0, (j - 1) * blk:j * blk, lanes]

    def head_lanes(hk):
        bd = slice(2 * hk * LANES, (2 * hk + 2) * LANES)
        pair_lanes = [slice((2 * hk + e) * LANES, (2 * hk + e + 1) * LANES) for e in range(2)]
        return bd, pair_lanes

    def score_pieces(hk):
        bd, pair_lanes = head_lanes(hk)
        buf = hk % 2

        def piece(j):
            def run():
                i0, i1 = max(j - 2, 0), min(j, qb - 1)
                kj = key_block(kp_ref, km_ref, kn_ref, j, bd)
                rhs = jnp.concatenate([kj[:, :LANES], kj[:, LANES:]], axis=0)
                lhs = jnp.concatenate([q_ref[0, i0 * blk:(i1 + 1) * blk, pl_] for pl_ in pair_lanes], axis=0)
                s = lax.dot_general(lhs, rhs, (((1,), (1,)), ((), ())), preferred_element_type=F32)
                nrow = (i1 - i0 + 1) * blk
                for e in range(2):
                    for i in range(i0, i1 + 1):
                        r0 = e * nrow + (i - i0) * blk
                        s_ref[buf, e, i, j - i] = s[r0:r0 + blk, :].astype(BF16)
            return run

        return [piece(j) for j in range(qb + 2)]

    def softmax_pieces(hk):
        _, pair_lanes = head_lanes(hk)
        buf = hk % 2

        def piece(e, i):
            def run():
                sink_a = sink_ref[SWA_GROUP * hk + 2 * e] * LOG2E
                sink_b = sink_ref[SWA_GROUP * hk + 2 * e + 1] * LOG2E
                g = n * qb + i
                tbl_prev = 0 if i > 0 else jnp.where(g > 0, 0, 2)
                tbl_next = 1 if i < qb - 1 else jnp.where(g < nq - 1, 1, 2)
                for r0 in range(0, blk, SWA_ROWS):
                    rr = slice(r0, r0 + SWA_ROWS)
                    sp = s_ref[buf, e, i, 0, rr, :] + bias_ref[tbl_prev, rr, :]
                    sc = s_ref[buf, e, i, 1, rr, :]
                    sn = s_ref[buf, e, i, 2, rr, :] + bias_ref[tbl_next, rr, :]
                    mx = jnp.maximum(jnp.maximum(sp, sc), sn).astype(F32)
                    m_a = jnp.maximum(jnp.max(mx[:, :LANES], axis=-1, keepdims=True), sink_a)
                    m_b = jnp.maximum(jnp.max(mx[:, LANES:], axis=-1, keepdims=True), sink_b)
                    m = jnp.where(head_a, m_a, m_b).astype(BF16)
                    probs = [jnp.exp2(sw - m) for sw in (sp, sc, sn)]
                    tot = (probs[0] + probs[1] + probs[2]).astype(F32)
                    m32 = m.astype(F32)
                    inv_a = 1.0 / (jnp.sum(tot[:, :LANES], axis=-1, keepdims=True)
                                   + jnp.exp2(sink_a - m32[:, :1]))
                    inv_b = 1.0 / (jnp.sum(tot[:, LANES:], axis=-1, keepdims=True)
                                   + jnp.exp2(sink_b - m32[:, LANES:LANES + 1]))
                    for which, pw in enumerate(probs):
                        slot = 2 - which
                        p_ref[buf, e, i + which, slot * blk + r0:slot * blk + r0 + SWA_ROWS, :] = pw
                    inv_ref[i * blk + r0:i * blk + r0 + SWA_ROWS, pair_lanes[e]] = jnp.where(low, inv_a, inv_b)
            return run

        return [piece(e, i) for e in range(2) for i in range(qb)]

    def value_pieces(hk):
        bd, pair_lanes = head_lanes(hk)
        buf = hk % 2

        def piece(j):
            def run():
                i0, i1 = max(j - 2, 0), min(j, qb - 1)
                vj = key_block(vp_ref, vm_ref, vn_ref, j, bd)
                rhs = jnp.concatenate([vj[:, :LANES], vj[:, LANES:]], axis=0)
                r0, r1 = (i0 - (j - 2)) * blk, (i1 - (j - 2) + 1) * blk
                lhs = jnp.concatenate([p_ref[buf, e, j, r0:r1, :] for e in range(2)], axis=0)
                o = jnp.dot(lhs, rhs, preferred_element_type=F32)
                nrow = r1 - r0
                for e in range(2):
                    for i in range(i0, i1 + 1):
                        part = o[e * nrow + (i - i0) * blk:e * nrow + (i - i0 + 1) * blk, :]
                        rows = slice(i * blk, (i + 1) * blk)
                        if j == i:
                            o_ref[rows, pair_lanes[e]] = part
                        else:
                            o_ref[rows, pair_lanes[e]] += part
            return run

        return [piece(j) for j in range(qb + 2)]

    for t in range(SWA_KV_HEADS + 2):
        mm = []
        if t < SWA_KV_HEADS:
            mm += score_pieces(t)
        if 0 <= t - 2 < SWA_KV_HEADS:
            mm += value_pieces(t - 2)
        ew = softmax_pieces(t - 1) if 0 <= t - 1 < SWA_KV_HEADS else []
        _run_interleaved(mm, ew)
    o_all = (o_ref[...] * inv_ref[...]).astype(BF16)
    y = jnp.dot(o_all, wout_ref[...], preferred_element_type=F32)
    y_ref[0] = x_ref[0] + _rms(y, gpost_ref[...])


def _swa_attn(q, k, v, x, sinks, g_post, w_out):
    bsz, seq, _ = x.shape
    nq = seq // SWA_BLOCK
    qb = SWA_QB
    tq = qb * SWA_BLOCK
    cur = lambda b, n: (b, n, 0)
    prev = lambda b, n: (b, jnp.maximum(n * qb - 1, 0), 0)
    nxt = lambda b, n: (b, jnp.minimum((n + 1) * qb, nq - 1), 0)
    halo = lambda im: pl.BlockSpec((1, SWA_BLOCK, SWA_BD), im)
    main = lambda w: pl.BlockSpec((1, tq, w), cur)
    return pl.pallas_call(
        functools.partial(_swa_attn_body, nq),
        grid=(bsz, nq // qb),
        in_specs=[
            pl.BlockSpec(memory_space=pltpu.SMEM),
            main(SWA_Q),
            halo(prev), main(SWA_BD), halo(nxt), halo(prev), main(SWA_BD), halo(nxt),
            main(D_MODEL),
            _const_spec((1, D_MODEL)),
            _const_spec((SWA_Q, D_MODEL)),
        ],
        out_specs=main(D_MODEL),
        out_shape=jax.ShapeDtypeStruct((bsz, seq, D_MODEL), F32),
        scratch_shapes=[
            pltpu.VMEM((2, 2, qb, 3, SWA_BLOCK, 2 * LANES), BF16),
            pltpu.VMEM((2, 2, qb + 2, 3 * SWA_BLOCK, 2 * LANES), BF16),
            pltpu.VMEM((tq, SWA_Q), F32),
            pltpu.VMEM((tq, SWA_Q), F32),
            pltpu.VMEM((3, SWA_BLOCK, 2 * LANES), BF16),
        ],
        compiler_params=_params(("parallel", "parallel")),
        name="swa_attn",
    )(sinks, q, k, k, k, v, v, v, x, g_post, w_out)


def _rope_tables(seq):
    inv_freq = ROPE_THETA ** (-(jnp.arange(ROPE_HALF, dtype=F32) * 2.0 / ROPE_DIM))
    ang = jnp.arange(seq, dtype=F32)[:, None] * inv_freq[None, :]
    cos8, sin8 = jnp.cos(ang), jnp.sin(ang)
    ones = jnp.ones((seq, SWA_HD - ROPE_DIM), F32)
    zeros8 = jnp.zeros((seq, ROPE_HALF), F32)
    zeros = jnp.zeros((seq, SWA_HD - ROPE_DIM), F32)
    cos = jnp.concatenate([cos8, cos8, ones], axis=1)
    sin_a = jnp.concatenate([-sin8, zeros8, zeros], axis=1)
    sin_b = jnp.concatenate([zeros8, sin8, zeros], axis=1)
    rep = LANES // SWA_HD
    return tuple(jnp.tile(t, (1, rep)) for t in (cos, sin_a, sin_b))


def _swa_layer(x, g_pre, g_post, w):
    seq = x.shape[1]
    cos, sin_a, sin_b = _rope_tables(seq)
    q, k, v = _swa_proj(x, g_pre, w["win"], cos, sin_a, sin_b)
    return _swa_attn(q, k, v, x, w["sinks"], g_post, w["wout"])


def _ffn_layer(x, g2, w1r, w2, layer, which, gla=None):
    bsz, seq, _ = x.shape
    return _ffn(x.reshape(bsz * seq, D_MODEL), g2, w1r, w2, layer, which, gla).reshape(bsz, seq, D_MODEL)


def _prep_weights(ffn_w1, ffn_w2, gla_w_in, gla_w_gate_f, gla_b_gate_f, gla_w_gate_b, gla_b_gate_b,
                  gla_onorm, gla_w_out, swa_w_in, swa_sinks, swa_w_out):
    w1r = ffn_w1.astype(BF16)
    w2 = ffn_w2.astype(BF16)
    gla = []
    for j in range(gla_w_in.shape[0]):
        n_main = 2 * GLA_QK + 2 * GLA_V
        wgd = jnp.zeros((D_MODEL, LANES), F32).at[:, :2 * GLA_GATE_RANK].set(gla_w_in[j][:, n_main:])
        wgate = jnp.zeros((LANES, 2 * GLA_QK), F32)
        wgate = wgate.at[:GLA_GATE_RANK, :GLA_QK].set(gla_w_gate_f[j])
        wgate = wgate.at[GLA_GATE_RANK:2 * GLA_GATE_RANK, GLA_QK:].set(gla_w_gate_b[j])
        gla.append(dict(
            wm=gla_w_in[j][:, :n_main].astype(BF16),
            wgd=wgd.astype(BF16),
            wgate=wgate.astype(BF16),
            bgate=jnp.concatenate([gla_b_gate_f[j], gla_b_gate_b[j]])[None, :],
            onorm=gla_onorm[j][None, :],
            wout=gla_w_out[j].astype(BF16),
        ))
    swa = []
    for j in range(swa_w_in.shape[0]):
        swa.append(dict(win=swa_w_in[j].astype(BF16), sinks=swa_sinks[j], wout=swa_w_out[j].astype(BF16)))
    return w1r, w2, gla, swa


def _trunk(x, norm_g, w1r, w2, gla, swa):
    for i in range(DEPTH):
        g = norm_g[i]
        x = _ffn_layer(x, g[0:2], w1r, w2, i, 0)
        if i % 2 == 0:
            w = gla[i // 2]
            o_f, o_b, gate = _gla_mixer(x, g[2:3], w)
            x = _ffn_layer(x, g[4:6], w1r, w2, i, 1, gla=(o_f, o_b, gate, w["onorm"], g[3:4], w["wout"]))
        else:
            x = _swa_layer(x, g[2:3], g[3:4], swa[i // 2])
            x = _ffn_layer(x, g[4:6], w1r, w2, i, 1)
    return x


def kernel(x_prompt, x_sample, norm_g, ffn_w1, ffn_w2, gla_w_in, gla_w_gate_f, gla_b_gate_f, gla_w_gate_b,
           gla_b_gate_b, gla_onorm, gla_w_out, swa_w_in, swa_sinks, swa_w_out):
    w1r, w2, gla, swa = _prep_weights(ffn_w1, ffn_w2, gla_w_in, gla_w_gate_f, gla_b_gate_f, gla_w_gate_b,
                                      gla_b_gate_b, gla_onorm, gla_w_out, swa_w_in, swa_sinks, swa_w_out)
    y_prompt = _trunk(x_prompt, norm_g, w1r, w2, gla, swa)
    y_sample = _trunk(x_sample, norm_g, w1r, w2, gla, swa)
    return (y_prompt, y_sample)
```
